```python
import math
import jax, jax.numpy as jnp
from jax import lax
import numpy as np

D_MODEL = 1024
BATCH = 16
SEQ = 2048
DEPTH = 4

MEM_LEN = 256
RMS_EPS = 1e-6
SB_HEADS = 16
SB_HEAD_DIM = 64
SB_WIDTH = SB_HEADS * SB_HEAD_DIM
SB_BLOCK = 128
SSM_EXPAND = 2
SSM_INNER = SSM_EXPAND * D_MODEL
SSM_HEAD_DIM = 64
SSM_HEADS = SSM_INNER // SSM_HEAD_DIM
SSM_GROUPS = 4
SSM_STATE = 128
SSM_CONV = 4
SSM_CHUNK = 128
SSM_CONV_DIM = SSM_INNER + 2 * SSM_GROUPS * SSM_STATE
XA_HEADS = 4
XA_HEAD_DIM = D_MODEL // XA_HEADS
FFN_HIDDEN = ((8 * D_MODEL + 767) // 768) * 256
IN_SIZES = (SB_WIDTH, SB_WIDTH, SB_WIDTH, SSM_INNER, SSM_CONV_DIM, SSM_HEADS, D_MODEL, D_MODEL)
IN_WIDTH = 3 * SB_WIDTH + SSM_INNER + SSM_CONV_DIM + SSM_HEADS + 2 * D_MODEL

kernel_name = "hybrid_stickbreak_ssd_gated_block"


def _split(a, sizes):
    idx = tuple(int(i) for i in np.cumsum(sizes)[:-1])
    return jnp.split(a, idx, axis=-1)


def rms_norm(x, g):
    xf = x.astype(jnp.float32)
    y = xf * lax.rsqrt(jnp.mean(xf * xf, axis=-1, keepdims=True) + RMS_EPS)
    return (y * g.astype(jnp.float32)).astype(x.dtype)


def stick_breaking_attention(q, k, v):
    bsz, seq = q.shape[0], q.shape[1]
    qf = jnp.swapaxes(q.astype(jnp.float32), 1, 2) * (SB_HEAD_DIM ** -0.5)
    kf = jnp.swapaxes(k.astype(jnp.float32), 1, 2)
    vf = jnp.swapaxes(v.astype(jnp.float32), 1, 2)
    outs = []
    for blk in range(seq // SB_BLOCK):
        t0 = blk * SB_BLOCK
        t1 = t0 + SB_BLOCK
        z = jnp.einsum('bhtd,bhsd->bhts', qf[:, :, t0:t1], kf[:, :, :t1])
        causal = jnp.arange(t1)[None, :] < (t0 + jnp.arange(SB_BLOCK))[:, None]
        log_1mb = jnp.where(causal, jax.nn.log_sigmoid(-z), 0.0)
        tail = lax.cumsum(log_1mb, axis=3, reverse=True)
        w = jnp.where(causal, jnp.exp(z + tail), 0.0)
        outs.append(jnp.einsum('bhts,bhsd->bhtd', w, vf[:, :, :t1]))
    o = jnp.concatenate(outs, axis=2)
    return jnp.swapaxes(o, 1, 2).reshape(bsz, seq, SB_WIDTH)


def ssd_chunked(x, dt, a, bm, cm):
    b, l, h, p = x.shape
    g, n = bm.shape[2], bm.shape[3]
    e = h // g
    c = l // SSM_CHUNK
    xc = (x * dt[..., None]).reshape(b, c, SSM_CHUNK, g, e, p)
    a_dt = (dt * a).reshape(b, c, SSM_CHUNK, g, e).transpose(0, 1, 3, 4, 2)
    a_cs = jnp.cumsum(a_dt, axis=-1)
    bc = bm.reshape(b, c, SSM_CHUNK, g, n)
    cc = cm.reshape(b, c, SSM_CHUNK, g, n)
    tril = jnp.tril(jnp.ones((SSM_CHUNK, SSM_CHUNK), dtype=bool))
    decay = jnp.exp(jnp.where(tril, a_cs[..., :, None] - a_cs[..., None, :], -jnp.inf))
    cb = jnp.einsum('bclgn,bcsgn->bcgls', cc, bc)
    y_diag = jnp.einsum('bcgls,bcgels,bcsgep->bclgep', cb, decay, xc)
    decay_to_end = jnp.exp(a_cs[..., -1:] - a_cs)
    states = jnp.einsum('bclgn,bcgel,bclgep->bcgepn', bc, decay_to_end, xc)
    chunk_decay = jnp.exp(a_cs[..., -1])

    def step(prev, inp):
        st, dec = inp
        return prev * dec[..., None, None] + st, prev

    init = jnp.zeros((b, g, e, p, n), dtype=states.dtype)
    _, prev_states = lax.scan(step, init, (jnp.moveaxis(states, 1, 0), jnp.moveaxis(chunk_decay, 1, 0)))
    prev_states = jnp.moveaxis(prev_states, 0, 1)
    y_off = jnp.einsum('bclgn,bcgepn,bcgel->bclgep', cc, prev_states, jnp.exp(a_cs))
    return (y_diag + y_off).reshape(b, l, h, p)


def ssd_branch(z, xbc, dt_raw, conv_w, conv_b, dt_bias, a_log, d_skip, g_norm):
    bsz, seq = xbc.shape[0], xbc.shape[1]
    xbc = lax.conv_general_dilated(
        xbc, conv_w[:, None, :].astype(xbc.dtype), window_strides=(1,),
        padding=[(SSM_CONV - 1, 0)], dimension_numbers=('NWC', 'WIO', 'NWC'),
        feature_group_count=SSM_CONV_DIM)
    xbc = jax.nn.silu(xbc.astype(jnp.float32) + conv_b.astype(jnp.float32))
    xs, bm, cm = _split(xbc, (SSM_INNER, SSM_GROUPS * SSM_STATE, SSM_GROUPS * SSM_STATE))
    dt = jax.nn.softplus(dt_raw.astype(jnp.float32) + dt_bias.astype(jnp.float32))
    a = -jnp.exp(a_log.astype(jnp.float32))
    xh = xs.reshape(bsz, seq, SSM_HEADS, SSM_HEAD_DIM)
    y = ssd_chunked(xh, dt, a,
                    bm.reshape(bsz, seq, SSM_GROUPS, SSM_STATE),
                    cm.reshape(bsz, seq, SSM_GROUPS, SSM_STATE))
    y = y + d_skip.astype(jnp.float32)[:, None] * xh
    y = y.reshape(bsz, seq, SSM_INNER) * jax.nn.silu(z.astype(jnp.float32))
    yg = y.reshape(bsz, seq, SSM_GROUPS, SSM_INNER // SSM_GROUPS)
    yg = yg * lax.rsqrt(jnp.mean(yg * yg, axis=-1, keepdims=True) + RMS_EPS)
    return yg.reshape(bsz, seq, SSM_INNER) * g_norm.astype(jnp.float32)


def memory_cross_attention(h, mem_n, w_xq, w_xkv, w_xo):
    bsz, seq = h.shape[0], h.shape[1]
    q = (h @ w_xq).reshape(bsz, seq, XA_HEADS, XA_HEAD_DIM)
    k, v = _split(mem_n @ w_xkv, (D_MODEL, D_MODEL))
    k = k.reshape(bsz, MEM_LEN, XA_HEADS, XA_HEAD_DIM)
    v = v.reshape(bsz, MEM_LEN, XA_HEADS, XA_HEAD_DIM)
    s = jnp.einsum('bshd,bmhd->bhsm', q.astype(jnp.float32), k.astype(jnp.float32)) * (XA_HEAD_DIM ** -0.5)
    p = jax.nn.softmax(s, axis=-1)
    o = jnp.einsum('bhsm,bmhd->bshd', p, v.astype(jnp.float32)).reshape(bsz, seq, D_MODEL)
    return o.astype(h.dtype) @ w_xo


def swiglu(h, w_gu, w_down):
    gate, up = _split(h @ w_gu, (FFN_HIDDEN, FFN_HIDDEN))
    return (jax.nn.silu(gate) * up) @ w_down


def setup_inputs(seed: int = 0) -> dict:
    key = jax.random.key(seed)
    ks = jax.random.split(key, 32)
    f32 = jnp.float32

    def dense(k, shape, fan_in):
        return jax.random.normal(k, shape, f32) * (fan_in ** -0.5)

    def gain(k, shape):
        return 1.0 + 0.02 * jax.random.normal(k, shape, f32)

    dt0 = jnp.exp(jax.random.uniform(ks[5], (DEPTH, SSM_HEADS), f32,
                                     minval=math.log(1e-3), maxval=math.log(1e-1)))
    dt_bias = dt0 + jnp.log(-jnp.expm1(-dt0))
    a_log = jnp.log(jax.random.uniform(ks[6], (DEPTH, SSM_HEADS), f32, minval=1.0, maxval=16.0))
    return {
        "x": jax.random.normal(ks[0], (BATCH, SEQ, D_MODEL), f32),
        "mem": jax.random.normal(ks[1], (BATCH, MEM_LEN, D_MODEL), f32),
        "g_pre_mix": gain(ks[2], (DEPTH, D_MODEL)),
        "w_in": dense(ks[3], (DEPTH, D_MODEL, IN_WIDTH), D_MODEL),
        "conv_w": dense(ks[4], (DEPTH, SSM_CONV, SSM_CONV_DIM), SSM_CONV),
        "conv_b": 0.01 * jax.random.normal(ks[7], (DEPTH, SSM_CONV_DIM), f32),
        "dt_bias": dt_bias,
        "a_log": a_log,
        "d_skip": 1.0 + 0.1 * jax.random.normal(ks[8], (DEPTH, SSM_HEADS), f32),
        "g_ssm_norm": gain(ks[9], (DEPTH, SSM_INNER)),
        "w_br_att": dense(ks[10], (DEPTH, SB_WIDTH, D_MODEL), SB_WIDTH),
        "w_br_ssm": dense(ks[11], (DEPTH, SSM_INNER, D_MODEL), SSM_INNER),
        "w_mix_out": dense(ks[12], (DEPTH, D_MODEL, D_MODEL), D_MODEL),
        "g_post_mix": gain(ks[13], (DEPTH, D_MODEL)),
        "g_pre_xa": gain(ks[14], (DEPTH, D_MODEL)),
        "g_mem": gain(ks[15], (DEPTH, D_MODEL)),
        "w_xq": dense(ks[16], (DEPTH, D_MODEL, D_MODEL), D_MODEL),
        "w_xkv": dense(ks[17], (DEPTH, D_MODEL, 2 * D_MODEL), D_MODEL),
        "w_xo": dense(ks[18], (DEPTH, D_MODEL, D_MODEL), D_MODEL),
        "g_post_xa": gain(ks[19], (DEPTH, D_MODEL)),
        "g_pre_ffn": gain(ks[20], (DEPTH, D_MODEL)),
        "w_gu": dense(ks[21], (DEPTH, D_MODEL, 2 * FFN_HIDDEN), D_MODEL),
        "w_down": dense(ks[22], (DEPTH, FFN_HIDDEN, D_MODEL), FFN_HIDDEN),
        "g_post_ffn": gain(ks[23], (DEPTH, D_MODEL)),
    }


def reference(x, mem, g_pre_mix, w_in, conv_w, conv_b, dt_bias, a_log, d_skip, g_ssm_norm,
              w_br_att, w_br_ssm, w_mix_out, g_post_mix, g_pre_xa, g_mem, w_xq, w_xkv, w_xo,
              g_post_xa, g_pre_ffn, w_gu, w_down, g_post_ffn):
    bsz, seq = x.shape[0], x.shape[1]
    for l in range(DEPTH):
        h = rms_norm(x, g_pre_mix[l])
        q, k, v, z, xbc, dt_raw, ga, gs = _split(h @ w_in[l], IN_SIZES)
        o_att = stick_breaking_attention(
            q.reshape(bsz, seq, SB_HEADS, SB_HEAD_DIM),
            k.reshape(bsz, seq, SB_HEADS, SB_HEAD_DIM),
            v.reshape(bsz, seq, SB_HEADS, SB_HEAD_DIM)).astype(x.dtype)
        o_ssm = ssd_branch(z, xbc, dt_raw, conv_w[l], conv_b[l], dt_bias[l], a_log[l],
                           d_skip[l], g_ssm_norm[l]).astype(x.dtype)
        merged = jax.nn.sigmoid(ga) * (o_att @ w_br_att[l]) + jax.nn.sigmoid(gs) * (o_ssm @ w_br_ssm[l])
        x = x + rms_norm(merged @ w_mix_out[l], g_post_mix[l])
        h = rms_norm(x, g_pre_xa[l])
        mem_n = rms_norm(mem, g_mem[l])
        x = x + rms_norm(memory_cross_attention(h, mem_n, w_xq[l], w_xkv[l], w_xo[l]), g_post_xa[l])
        h = rms_norm(x, g_pre_ffn[l])
        x = x + rms_norm(swiglu(h, w_gu[l], w_down[l]), g_post_ffn[l])
    return x
```

```python
import functools

import jax
import jax.numpy as jnp
from jax import lax
from jax.experimental import pallas as pl
from jax.experimental.pallas import tpu as pltpu

F32 = jnp.float32
BF16 = jnp.bfloat16

RMS_EPS = 1e-6
SB_HEAD_DIM = 64
SB_BLOCK = 128
SSM_HEAD_DIM = 64
SSM_GROUPS = 4
SSM_STATE = 128
SSM_CONV = 4
SSM_CHUNK = 128
XA_HEADS = 4
LANES = 128
VMEM_LIMIT_BYTES = 56 * 1024 * 1024


def _cparams(*sem):
    return pltpu.CompilerParams(dimension_semantics=sem, vmem_limit_bytes=VMEM_LIMIT_BYTES)


def _pick_tile(n, want):
    t = min(n, want)
    while n % t:
        t //= 2
    return t


def _resident(shape):
    return pl.BlockSpec(shape, lambda *_: (0,) * len(shape), pipeline_mode=pl.Buffered(1))


def _rms(x, g):
    ms = jnp.mean(x * x, axis=-1, keepdims=True)
    return x * lax.rsqrt(ms + RMS_EPS) * g


def _sigmoid(x):
    return 1.0 / (1.0 + jnp.exp(-x))


def _split3(x):
    hi = x.astype(BF16)
    r = x - hi.astype(F32)
    mid = r.astype(BF16)
    lo = (r - mid.astype(F32)).astype(BF16)
    return hi, mid, lo


def _inproj_kernel(x_ref, g_ref, w_ref, wdt_ref, wdtt_ref, o_ref, dt_ref, dtt_ref, h_ref):
    @pl.when(pl.program_id(1) == 0)
    def _():
        h = _rms(x_ref[...], g_ref[...]).astype(BF16)
        h_ref[...] = h
        dt_ref[...] = jnp.dot(h, wdt_ref[...], preferred_element_type=F32)
        dtt_ref[...] = lax.dot_general(wdtt_ref[...], h, (((1,), (1,)), ((), ())),
                                       preferred_element_type=F32)

    o_ref[...] = jnp.dot(h_ref[...], w_ref[...], preferred_element_type=F32).astype(o_ref.dtype)


def _inproj(x2, g, w, wdt, wdtt, tm, tn):
    t, d = x2.shape
    n = w.shape[1]
    nh = wdtt.shape[0]
    return pl.pallas_call(
        _inproj_kernel,
        grid=(t // tm, n // tn),
        in_specs=[
            pl.BlockSpec((tm, d), lambda i, j: (i, 0)),
            pl.BlockSpec((1, d), lambda i, j: (0, 0)),
            pl.BlockSpec((d, tn), lambda i, j: (0, j)),
            pl.BlockSpec((d, LANES), lambda i, j: (0, 0)),
            pl.BlockSpec((nh, d), lambda i, j: (0, 0)),
        ],
        out_specs=[
            pl.BlockSpec((tm, tn), lambda i, j: (i, j)),
            pl.BlockSpec((tm, LANES), lambda i, j: (i, 0)),
            pl.BlockSpec((nh, tm), lambda i, j: (0, i)),
        ],
        out_shape=[
            jax.ShapeDtypeStruct((t, n), BF16),
            jax.ShapeDtypeStruct((t, LANES), F32),
            jax.ShapeDtypeStruct((nh, t), F32),
        ],
        scratch_shapes=[pltpu.VMEM((tm, d), BF16)],
        compiler_params=_cparams("parallel", "arbitrary"),
        name="inproj",
    )(x2, g, w, wdt, wdtt)


def _norm_matmul_kernel(x_ref, g_ref, w_ref, o_ref):
    h = _rms(x_ref[...], g_ref[...]).astype(BF16)
    o_ref[...] = jnp.dot(h, w_ref[...], preferred_element_type=F32).astype(o_ref.dtype)


def _norm_matmul(x2, g, w, tm):
    t, d = x2.shape
    n = w.shape[1]
    return pl.pallas_call(
        _norm_matmul_kernel,
        grid=(t // tm,),
        in_specs=[
            pl.BlockSpec((tm, d), lambda i: (i, 0)),
            pl.BlockSpec((1, d), lambda i: (0, 0)),
            _resident((d, n)),
        ],
        out_specs=pl.BlockSpec((tm, n), lambda i: (i, 0)),
        out_shape=jax.ShapeDtypeStruct((t, n), BF16),
        compiler_params=_cparams("parallel"),
        name="mem_kv",
    )(x2, g, w)


def _sb_kernel(q_ref, k_ref, v_ref, uu_ref, o_ref, acc_ref, run_ref):
    i = pl.program_id(2)
    blk = SB_BLOCK
    lane = lax.broadcasted_iota(jnp.int32, (blk, LANES), 1)
    first_head = lane < SB_HEAD_DIM
    q = q_ref[...] * (SB_HEAD_DIM ** -0.5)
    zero = jnp.zeros_like(q)
    qs = jnp.concatenate([jnp.where(first_head, q, zero), jnp.where(first_head, zero, q)], axis=0)
    row = lax.broadcasted_iota(jnp.int32, (2 * blk, blk), 0)
    col = lax.broadcasted_iota(jnp.int32, (2 * blk, blk), 1)
    causal = col < jnp.where(row >= blk, row - blk, row)

    def block(j, diag):
        start = pl.multiple_of(j * blk, blk)
        kb = k_ref[pl.ds(start, blk), :]
        vb = v_ref[pl.ds(start, blk), :]
        z = lax.dot_general(qs, kb, (((1,), (1,)), ((), ())), preferred_element_type=F32)
        log_1mb = -(jnp.maximum(z, 0.0) + jnp.log(1.0 + jnp.exp(-jnp.abs(z))))
        if diag:
            log_1mb = jnp.where(causal, log_1mb, 0.0)
        hi = log_1mb.astype(BF16)
        lo = (log_1mb - hi.astype(F32)).astype(BF16)
        c = jnp.dot(jnp.concatenate([hi, lo], axis=1), uu_ref[...], preferred_element_type=F32)
        if diag:
            tail = c[:, :blk]
            run_ref[...] = c[:, blk:]
        else:
            run = run_ref[...]
            tail = c[:, :blk] + run
            run_ref[...] = run + c[:, blk:]
        w = jnp.exp(z + tail)
        if diag:
            w = jnp.where(causal, w, 0.0)
        pv = jnp.dot(w.astype(BF16), vb, preferred_element_type=F32)
        if diag:
            acc_ref[...] = pv
        else:
            acc_ref[...] += pv

    block(i, True)

    def body(jj, carry):
        block(i - 1 - jj, False)
        return carry

    lax.fori_loop(0, i, body, 0)
    acc = acc_ref[...]
    o_ref[...] = jnp.where(first_head, acc[:blk], acc[blk:]).astype(o_ref.dtype)


def _sb_attention(proj, uu, bsz, seq, n_pairs, q_col, k_col, v_col):
    nq = seq // SB_BLOCK
    spb = seq // SB_BLOCK
    return pl.pallas_call(
        _sb_kernel,
        grid=(bsz, n_pairs, nq),
        in_specs=[
            pl.BlockSpec((SB_BLOCK, LANES), lambda b, p, i: (b * spb + i, q_col + p)),
            pl.BlockSpec((seq, LANES), lambda b, p, i: (b, k_col + p)),
            pl.BlockSpec((seq, LANES), lambda b, p, i: (b, v_col + p)),
            _resident((2 * SB_BLOCK, 2 * SB_BLOCK)),
        ],
        out_specs=pl.BlockSpec((SB_BLOCK, LANES), lambda b, p, i: (b * spb + i, p)),
        out_shape=jax.ShapeDtypeStruct((bsz * seq, n_pairs * LANES), BF16),
        scratch_shapes=[pltpu.VMEM((2 * SB_BLOCK, LANES), F32),
                        pltpu.VMEM((2 * SB_BLOCK, SB_BLOCK), F32)],
        compiler_params=_cparams("parallel", "parallel", "arbitrary"),
        name="sb_attention",
    )(proj, proj, proj, uu)


def _ssd_kernel(xbc_ref, z_ref, dt_ref, dtt_ref, convw_ref, convb_ref, dtb_ref, dtbt_ref,
                alog_ref, alogt_ref, dskip_ref, gnorm_ref, tri_ref, trit_ref,
                o_ref, state_ref, halo_ref, *, inner, n_heads):
    c = pl.program_id(1)
    cl = SSM_CHUNK
    gw = SSM_GROUPS * SSM_STATE
    hpg = n_heads // SSM_GROUPS

    @pl.when(c == 0)
    def _():
        state_ref[...] = jnp.zeros_like(state_ref)
        halo_ref[...] = jnp.zeros_like(halo_ref)

    xin = xbc_ref[...].astype(F32)
    halo = halo_ref[...]
    row8 = lax.broadcasted_iota(jnp.int32, halo.shape, 0)
    conv = xin * convw_ref[SSM_CONV - 1:SSM_CONV, :] + convb_ref[...]
    for k in range(1, SSM_CONV):
        cur = pltpu.roll(xin, k, 0)
        top = jnp.where(row8 < k, pltpu.roll(halo, k, 0), cur[:8])
        shifted = jnp.concatenate([top, cur[8:]], axis=0)
        conv = conv + shifted * convw_ref[SSM_CONV - 1 - k:SSM_CONV - k, :]
    halo_ref[...] = xin[cl - 8:, :]
    act = conv * _sigmoid(conv)
    xs = act[:, :inner]
    bm = act[:, inner:inner + gw]
    cm = act[:, inner + gw:]

    def softplus(v):
        return jnp.maximum(v, 0.0) + jnp.log(1.0 + jnp.exp(-jnp.abs(v)))

    dt_r = softplus(dt_ref[...] + dtb_ref[...])
    dt_t = softplus(dtt_ref[...] + dtbt_ref[...])
    adt_r = dt_r * (-jnp.exp(alog_ref[...]))
    adt_t = dt_t * (-jnp.exp(alogt_ref[...]))
    acs_r = sum(jnp.dot(tri_ref[...], p, preferred_element_type=F32) for p in _split3(adt_r))
    acs_t = sum(jnp.dot(p, trit_ref[...], preferred_element_type=F32) for p in _split3(adt_t))

    lane = lax.broadcasted_iota(jnp.int32, (cl, LANES), 1)
    first_head = lane < SSM_HEAD_DIM
    tl = lax.broadcasted_iota(jnp.int32, (cl, cl), 0)
    ts = lax.broadcasted_iota(jnp.int32, (cl, cl), 1)
    tril = ts <= tl

    y_pairs = []
    for g in range(SSM_GROUPS):
        bg = bm[:, g * SSM_STATE:(g + 1) * SSM_STATE]
        cg = cm[:, g * SSM_STATE:(g + 1) * SSM_STATE]
        cb = lax.dot_general(cg.astype(BF16), bg.astype(BF16), (((1,), (1,)), ((), ())),
                             preferred_element_type=F32)
        bgt = bg.T
        for pi in range(hpg // 2):
            h0 = g * hpg + 2 * pi
            lhs_y, lhs_s, cdec = [], [], []
            for h in (h0, h0 + 1):
                a_col = acs_r[:, h:h + 1]
                a_row = acs_t[h:h + 1, :]
                dt_row = dt_t[h:h + 1, :]
                a_end = acs_t[h:h + 1, cl - 1:cl]
                decay = jnp.exp(jnp.where(tril, a_col - a_row, -jnp.inf))
                m = cb * decay * dt_row
                c_scaled = cg * jnp.exp(a_col)
                lhs_y.append(jnp.concatenate([m.astype(BF16), c_scaled.astype(BF16)], axis=1))
                w_row = jnp.exp(a_end - a_row) * dt_row
                lhs_s.append((bgt * w_row).astype(BF16))
                cdec.append(jnp.exp(a_end))
            cols = slice(h0 * SSM_HEAD_DIM, (h0 + 2) * SSM_HEAD_DIM)
            x_pair = xs[:, cols].astype(BF16)
            s_prev = state_ref[:, cols]
            rhs = jnp.concatenate([x_pair, s_prev.astype(BF16)], axis=0)
            out = jnp.dot(jnp.concatenate(lhs_y, axis=0), rhs, preferred_element_type=F32)
            y_pairs.append(jnp.where(first_head, out[:cl], out[cl:]))
            st = jnp.dot(jnp.concatenate(lhs_s, axis=0), x_pair, preferred_element_type=F32)
            st = jnp.where(first_head, st[:SSM_STATE], st[SSM_STATE:])
            state_ref[:, cols] = s_prev * jnp.where(first_head, cdec[0], cdec[1]) + st

    y = jnp.concatenate(y_pairs, axis=1) + dskip_ref[...] * xs
    zf = z_ref[...].astype(F32)
    y = y * (zf * _sigmoid(zf))
    gsz = inner // SSM_GROUPS
    outs = []
    for g in range(SSM_GROUPS):
        yg = y[:, g * gsz:(g + 1) * gsz]
        outs.append(yg * lax.rsqrt(jnp.mean(yg * yg, axis=-1, keepdims=True) + RMS_EPS))
    o_ref[...] = (jnp.concatenate(outs, axis=1) * gnorm_ref[...]).astype(o_ref.dtype)


def _ssd(proj, dt, dtt, prm, bsz, seq, inner, n_heads, xbc_col, z_col):
    nc = seq // SSM_CHUNK
    cw = inner + 2 * SSM_GROUPS * SSM_STATE
    const = lambda shape: pl.BlockSpec(shape, lambda b, c: (0, 0))
    kern = functools.partial(_ssd_kernel, inner=inner, n_heads=n_heads)
    return pl.pallas_call(
        kern,
        grid=(bsz, nc),
        in_specs=[
            pl.BlockSpec((SSM_CHUNK, cw), lambda b, c: (b * nc + c, xbc_col)),
            pl.BlockSpec((SSM_CHUNK, inner), lambda b, c: (b * nc + c, z_col)),
            pl.BlockSpec((SSM_CHUNK, LANES), lambda b, c: (b * nc + c, 0)),
            pl.BlockSpec((n_heads, SSM_CHUNK), lambda b, c: (0, b * nc + c)),
            const((SSM_CONV, cw)),
            const((1, cw)),
            const((1, LANES)),
            const((n_heads, SSM_CHUNK)),
            const((1, LANES)),
            const((n_heads, SSM_CHUNK)),
            const((1, inner)),
            const((1, inner)),
            const((SSM_CHUNK, SSM_CHUNK)),
            const((SSM_CHUNK, SSM_CHUNK)),
        ],
        out_specs=pl.BlockSpec((SSM_CHUNK, inner), lambda b, c: (b * nc + c, 0)),
        out_shape=jax.ShapeDtypeStruct((bsz * seq, inner), BF16),
        scratch_shapes=[pltpu.VMEM((SSM_STATE, inner), F32), pltpu.VMEM((8, cw), F32)],
        compiler_params=_cparams("parallel", "arbitrary"),
        name="ssd",
    )(proj, proj, dt, dtt, prm["conv_w"], prm["conv_b"], prm["dt_bias_row"], prm["dt_bias_col"],
      prm["a_log_row"], prm["a_log_col"], prm["d_skip"], prm["g_norm"], prm["tri"], prm["trit"])


def _merge_kernel(x_ref, oa_ref, os_ref, ga_ref, gs_ref, wa_ref, ws_ref, wm_ref, g_ref, o_ref):
    a = jnp.dot(oa_ref[...], wa_ref[...], preferred_element_type=F32)
    s = jnp.dot(os_ref[...], ws_ref[...], preferred_element_type=F32)
    merged = _sigmoid(ga_ref[...].astype(F32)) * a + _sigmoid(gs_ref[...].astype(F32)) * s
    y = jnp.dot(merged.astype(BF16), wm_ref[...], preferred_element_type=F32)
    o_ref[...] = x_ref[...] + _rms(y, g_ref[...])


def _merge(x2, o_att, o_ssm, proj, wa, ws, wm, g, tm, ga_col, gs_col):
    t, d = x2.shape
    inner = o_ssm.shape[1]
    sbw = o_att.shape[1]
    return pl.pallas_call(
        _merge_kernel,
        grid=(t // tm,),
        in_specs=[
            pl.BlockSpec((tm, d), lambda i: (i, 0)),
            pl.BlockSpec((tm, sbw), lambda i: (i, 0)),
            pl.BlockSpec((tm, inner), lambda i: (i, 0)),
            pl.BlockSpec((tm, d), lambda i: (i, ga_col)),
            pl.BlockSpec((tm, d), lambda i: (i, gs_col)),
            _resident((sbw, d)),
            _resident((inner, d)),
            _resident((d, d)),
            _resident((1, d)),
        ],
        out_specs=pl.BlockSpec((tm, d), lambda i: (i, 0)),
        out_shape=jax.ShapeDtypeStruct((t, d), F32),
        compiler_params=_cparams("parallel"),
        name="merge",
    )(x2, o_att, o_ssm, proj, proj, wa, ws, wm, g)


def _xattn_kernel(x_ref, kv_ref, gpre_ref, wq_ref, wo_ref, gpost_ref, o_ref, *, d):
    x = x_ref[...]
    h = _rms(x, gpre_ref[...]).astype(BF16)
    hd = d // XA_HEADS
    q = (jnp.dot(h, wq_ref[...], preferred_element_type=F32) * (hd ** -0.5)).astype(BF16)
    outs = []
    for n in range(XA_HEADS):
        kh = kv_ref[:, n * hd:(n + 1) * hd]
        vh = kv_ref[:, d + n * hd:d + (n + 1) * hd]
        s = lax.dot_general(q[:, n * hd:(n + 1) * hd], kh, (((1,), (1,)), ((), ())),
                            preferred_element_type=F32)
        p = jnp.exp(s - jnp.max(s, axis=-1, keepdims=True))
        denom = jnp.sum(p, axis=-1, keepdims=True)
        o = jnp.dot(p.astype(BF16), vh, preferred_element_type=F32)
        outs.append((o / denom).astype(BF16))
    o = jnp.concatenate(outs, axis=1)
    y = jnp.dot(o, wo_ref[...], preferred_element_type=F32)
    o_ref[...] = x + _rms(y, gpost_ref[...])


def _xattn(x2, kv, gpre, wq, wo, gpost, tm, seq, mem_len):
    t, d = x2.shape
    per_b = seq // tm
    return pl.pallas_call(
        functools.partial(_xattn_kernel, d=d),
        grid=(t // tm,),
        in_specs=[
            pl.BlockSpec((tm, d), lambda i: (i, 0)),
            pl.BlockSpec((mem_len, 2 * d), lambda i: (i // per_b, 0)),
            _resident((1, d)),
            _resident((d, d)),
            _resident((d, d)),
            _resident((1, d)),
        ],
        out_specs=pl.BlockSpec((tm, d), lambda i: (i, 0)),
        out_shape=jax.ShapeDtypeStruct((t, d), F32),
        compiler_params=_cparams("parallel"),
        name="xattn",
    )(x2, kv, gpre, wq, wo, gpost)


def _ffn_kernel(x_ref, gpre_ref, wgu_ref, wd_ref, gpost_ref, o_ref, *, hidden, chunk):
    x = x_ref[...]
    h = _rms(x, gpre_ref[...]).astype(BF16)
    y = jnp.zeros(x.shape, F32)
    for c0 in range(0, hidden, chunk):
        gate = jnp.dot(h, wgu_ref[:, c0:c0 + chunk], preferred_element_type=F32)
        up = jnp.dot(h, wgu_ref[:, hidden + c0:hidden + c0 + chunk], preferred_element_type=F32)
        act = (gate * _sigmoid(gate) * up).astype(BF16)
        y = y + jnp.dot(act, wd_ref[c0:c0 + chunk, :], preferred_element_type=F32)
    o_ref[...] = x + _rms(y, gpost_ref[...])


def _ffn(x2, gpre, wgu, wd, gpost, tm):
    t, d = x2.shape
    hidden = wd.shape[0]
    chunk = hidden
    for cand in (512, 384, 256, 128):
        if hidden % cand == 0:
            chunk = cand
            break
    return pl.pallas_call(
        functools.partial(_ffn_kernel, hidden=hidden, chunk=chunk),
        grid=(t // tm,),
        in_specs=[
            pl.BlockSpec((tm, d), lambda i: (i, 0)),
            _resident((1, d)),
            _resident((d, 2 * hidden)),
            _resident((hidden, d)),
            _resident((1, d)),
        ],
        out_specs=pl.BlockSpec((tm, d), lambda i: (i, 0)),
        out_shape=jax.ShapeDtypeStruct((t, d), F32),
        compiler_params=_cparams("parallel"),
        name="ffn",
    )(x2, gpre, wgu, wd, gpost)


def _segment_layout(widths, tile):
    offs, cur = [], 0
    for w in widths:
        cur = -(-cur // w) * w
        offs.append(cur)
        cur += w
    return offs, -(-cur // tile) * tile


def kernel(x, mem, g_pre_mix, w_in, conv_w, conv_b, dt_bias, a_log, d_skip, g_ssm_norm, w_br_att,
           w_br_ssm, w_mix_out, g_post_mix, g_pre_xa, g_mem, w_xq, w_xkv, w_xo, g_post_xa,
           g_pre_ffn, w_gu, w_down, g_post_ffn):
    bsz, seq, d = x.shape
    depth = w_in.shape[0]
    mem_len = mem.shape[1]
    sbw = w_br_att.shape[1]
    inner = w_br_ssm.shape[1]
    n_heads = inner // SSM_HEAD_DIM
    gw = SSM_GROUPS * SSM_STATE
    cw = inner + 2 * gw
    t = bsz * seq
    assert seq % SB_BLOCK == 0 and seq % SSM_CHUNK == 0 and sbw % LANES == 0
    assert n_heads % (2 * SSM_GROUPS) == 0 and n_heads <= LANES and (d // XA_HEADS) % LANES == 0

    src = {}
    cur = 0
    for name, wd_ in (("q", sbw), ("k", sbw), ("v", sbw), ("z", inner), ("xbc", cw),
                      ("dt", n_heads), ("ga", d), ("gs", d)):
        src[name] = (cur, wd_)
        cur += wd_
    order = ("q", "k", "v", "xbc", "z", "ga", "gs")
    tn = _pick_tile(1 << 30, 1024)
    offs, n_total = _segment_layout([src[k][1] for k in order], tn)
    col = dict(zip(order, offs))

    tm_proj = _pick_tile(t, 1024)
    tm_row = _pick_tile(seq, 512)

    ti = lax.broadcasted_iota(jnp.int32, (SSM_CHUNK, SSM_CHUNK), 0)
    tj = lax.broadcasted_iota(jnp.int32, (SSM_CHUNK, SSM_CHUNK), 1)
    tri = (tj <= ti).astype(BF16)
    trit = (ti <= tj).astype(BF16)
    suffix = (ti >= tj).astype(BF16)
    half = jnp.concatenate([suffix, jnp.ones((SB_BLOCK, SB_BLOCK), BF16)], axis=1)
    uu = jnp.concatenate([half, half], axis=0)

    x2 = x.reshape(t, d)
    mem2 = mem.reshape(bsz * mem_len, d)
    row = lambda v: v.reshape(1, -1)
    for l in range(depth):
        w_full = w_in[l]
        w_perm = jnp.zeros((d, n_total), BF16)
        for k in order:
            s0, wd_ = src[k]
            w_perm = lax.dynamic_update_slice(w_perm, w_full[:, s0:s0 + wd_].astype(BF16), (0, col[k]))
        s0, wd_ = src["dt"]
        w_dt = w_full[:, s0:s0 + wd_].astype(BF16)
        w_dt_pad = jnp.zeros((d, LANES), BF16).at[:, :n_heads].set(w_dt)
        proj, dt, dtt = _inproj(x2, row(g_pre_mix[l]), w_perm, w_dt_pad, w_dt.T, tm_proj, tn)

        o_att = _sb_attention(proj, uu, bsz, seq, sbw // LANES,
                              col["q"] // LANES, col["k"] // LANES, col["v"] // LANES)
        pad_row = lambda v: jnp.zeros((1, LANES), F32).at[0, :n_heads].set(v)
        prm = {
            "conv_w": conv_w[l], "conv_b": row(conv_b[l]),
            "dt_bias_row": pad_row(dt_bias[l]),
            "dt_bias_col": jnp.broadcast_to(dt_bias[l][:, None], (n_heads, SSM_CHUNK)),
            "a_log_row": pad_row(a_log[l]),
            "a_log_col": jnp.broadcast_to(a_log[l][:, None], (n_heads, SSM_CHUNK)),
            "d_skip": row(jnp.repeat(d_skip[l], SSM_HEAD_DIM)),
            "g_norm": row(g_ssm_norm[l]),
            "tri": tri, "trit": trit,
        }
        o_ssm = _ssd(proj, dt, dtt, prm, bsz, seq, inner, n_heads, col["xbc"] // cw, col["z"] // inner)
        x2 = _merge(x2, o_att, o_ssm, proj, w_br_att[l].astype(BF16), w_br_ssm[l].astype(BF16),
                    w_mix_out[l].astype(BF16), row(g_post_mix[l]), tm_row,
                    col["ga"] // d, col["gs"] // d)

        kv = _norm_matmul(mem2, row(g_mem[l]), w_xkv[l].astype(BF16), _pick_tile(bsz * mem_len, 512))
        x2 = _xattn(x2, kv, row(g_pre_xa[l]), w_xq[l].astype(BF16), w_xo[l].astype(BF16),
                    row(g_post_xa[l]), tm_row, seq, mem_len)

        x2 = _ffn(x2, row(g_pre_ffn[l]), w_gu[l].astype(BF16), w_down[l].astype(BF16),
                  row(g_post_ffn[l]), tm_row)
    return x2.reshape(bsz, seq, d)
```

```python
import functools

import jax
import jax.numpy as jnp
from jax import lax
from jax.experimental import pallas as pl
from jax.experimental.pallas import tpu as pltpu

F32 = jnp.float32
BF16 = jnp.bfloat16

RMS_EPS = 1e-6
SB_HEAD_DIM = 64
SB_BLOCK = 128
SB_PAIRS_PER_STEP = 8
SSM_HEAD_DIM = 64
SSM_GROUPS = 4
SSM_STATE = 128
SSM_CONV = 4
SSM_CHUNK = 128
XA_HEADS = 4
LANES = 128
VMEM_LIMIT_BYTES = 56 * 1024 * 1024


def _cparams(*sem):
    return pltpu.CompilerParams(dimension_semantics=sem, vmem_limit_bytes=VMEM_LIMIT_BYTES)


def _pick_tile(n, want):
    t = min(n, want)
    while n % t:
        t //= 2
    return t


def _resident(shape):
    return pl.BlockSpec(shape, lambda *_: (0,) * len(shape), pipeline_mode=pl.Buffered(1))


def _rms(x, g):
    ms = jnp.mean(x * x, axis=-1, keepdims=True)
    return x * lax.rsqrt(ms + RMS_EPS) * g


def _sigmoid(x):
    return 1.0 / (1.0 + jnp.exp(-x))


def _split3(x):
    hi = x.astype(BF16)
    r = x - hi.astype(F32)
    mid = r.astype(BF16)
    lo = (r - mid.astype(F32)).astype(BF16)
    return hi, mid, lo


def _inproj_kernel(x_ref, g_ref, w_ref, wdt_ref, wdtt_ref, o_ref, dt_ref, dtt_ref, h_ref):
    @pl.when(pl.program_id(1) == 0)
    def _():
        h = _rms(x_ref[...], g_ref[...]).astype(BF16)
        h_ref[...] = h
        dt_ref[...] = jnp.dot(h, wdt_ref[...], preferred_element_type=F32)
        dtt_ref[...] = lax.dot_general(wdtt_ref[...], h, (((1,), (1,)), ((), ())),
                                       preferred_element_type=F32)

    o_ref[...] = jnp.dot(h_ref[...], w_ref[...], preferred_element_type=F32).astype(o_ref.dtype)


def _inproj(x2, g, w, wdt, wdtt, tm, tn):
    t, d = x2.shape
    n = w.shape[1]
    nh = wdtt.shape[0]
    return pl.pallas_call(
        _inproj_kernel,
        grid=(t // tm, n // tn),
        in_specs=[
            pl.BlockSpec((tm, d), lambda i, j: (i, 0)),
            pl.BlockSpec((1, d), lambda i, j: (0, 0)),
            pl.BlockSpec((d, tn), lambda i, j: (0, j)),
            pl.BlockSpec((d, LANES), lambda i, j: (0, 0)),
            pl.BlockSpec((nh, d), lambda i, j: (0, 0)),
        ],
        out_specs=[
            pl.BlockSpec((tm, tn), lambda i, j: (i, j)),
            pl.BlockSpec((tm, LANES), lambda i, j: (i, 0)),
            pl.BlockSpec((nh, tm), lambda i, j: (0, i)),
        ],
        out_shape=[
            jax.ShapeDtypeStruct((t, n), BF16),
            jax.ShapeDtypeStruct((t, LANES), F32),
            jax.ShapeDtypeStruct((nh, t), F32),
        ],
        scratch_shapes=[pltpu.VMEM((tm, d), BF16)],
        compiler_params=_cparams("parallel", "arbitrary"),
        name="inproj",
    )(x2, g, w, wdt, wdtt)


def _norm_matmul_kernel(x_ref, g_ref, w_ref, o_ref):
    h = _rms(x_ref[...], g_ref[...]).astype(BF16)
    o_ref[...] = jnp.dot(h, w_ref[...], preferred_element_type=F32).astype(o_ref.dtype)


def _norm_matmul(x2, g, w, tm):
    t, d = x2.shape
    n = w.shape[1]
    return pl.pallas_call(
        _norm_matmul_kernel,
        grid=(t // tm,),
        in_specs=[
            pl.BlockSpec((tm, d), lambda i: (i, 0)),
            pl.BlockSpec((1, d), lambda i: (0, 0)),
            _resident((d, n)),
        ],
        out_specs=pl.BlockSpec((tm, n), lambda i: (i, 0)),
        out_shape=jax.ShapeDtypeStruct((t, n), BF16),
        compiler_params=_cparams("parallel"),
        name="mem_kv",
    )(x2, g, w)


def _sb_kernel(q_ref, k_ref, v_ref, uu_ref, o_ref, qs_ref, acc_ref, run_ref, *, n_pairs):
    i = pl.program_id(2)
    blk = SB_BLOCK
    lane = lax.broadcasted_iota(jnp.int32, (blk, LANES), 1)
    first_head = lane < SB_HEAD_DIM
    row = lax.broadcasted_iota(jnp.int32, (2 * blk, blk), 0)
    col = lax.broadcasted_iota(jnp.int32, (2 * blk, blk), 1)
    causal = col < jnp.where(row >= blk, row - blk, row)

    for p in range(n_pairs):
        q = q_ref[:, p * LANES:(p + 1) * LANES] * (SB_HEAD_DIM ** -0.5)
        zero = jnp.zeros_like(q)
        qs_ref[p] = jnp.concatenate([jnp.where(first_head, q, zero), jnp.where(first_head, zero, q)],
                                    axis=0)

    def blocks(j, diag):
        start = pl.multiple_of(j * blk, blk)
        pairs = range(n_pairs)
        zs = [lax.dot_general(qs_ref[p], k_ref[pl.ds(start, blk), p * LANES:(p + 1) * LANES],
                              (((1,), (1,)), ((), ())), preferred_element_type=F32) for p in pairs]
        cs = []
        for p in pairs:
            z = zs[p]
            log_1mb = -(jnp.maximum(z, 0.0) + jnp.log(1.0 + jnp.exp(-jnp.abs(z))))
            if diag:
                log_1mb = jnp.where(causal, log_1mb, 0.0)
            hi = log_1mb.astype(BF16)
            lo = (log_1mb - hi.astype(F32)).astype(BF16)
            cs.append(jnp.dot(jnp.concatenate([hi, lo], axis=1), uu_ref[...],
                              preferred_element_type=F32))
        ws = []
        for p in pairs:
            c = cs[p]
            if diag:
                tail = c[:, :blk]
                run_ref[p] = c[:, blk:]
            else:
                run = run_ref[p]
                tail = c[:, :blk] + run
                run_ref[p] = run + c[:, blk:]
            w = jnp.exp(zs[p] + tail)
            if diag:
                w = jnp.where(causal, w, 0.0)
            ws.append(w.astype(BF16))
        for p in pairs:
            pv = jnp.dot(ws[p], v_ref[pl.ds(start, blk), p * LANES:(p + 1) * LANES],
                         preferred_element_type=F32)
            if diag:
                acc_ref[p] = pv
            else:
                acc_ref[p] += pv

    blocks(i, True)

    def body(jj, carry):
        blocks(i - 1 - jj, False)
        return carry

    lax.fori_loop(0, i, body, 0)
    for p in range(n_pairs):
        acc = acc_ref[p]
        o_ref[:, p * LANES:(p + 1) * LANES] = jnp.where(first_head, acc[:blk], acc[blk:]).astype(o_ref.dtype)


def _sb_attention(proj, uu, bsz, seq, n_pairs, pairs_per_step, q_col, k_col, v_col):
    nq = seq // SB_BLOCK
    gp = pairs_per_step
    width = gp * LANES
    return pl.pallas_call(
        functools.partial(_sb_kernel, n_pairs=gp),
        grid=(bsz, n_pairs // gp, nq),
        in_specs=[
            pl.BlockSpec((SB_BLOCK, width), lambda b, p, i: (b * nq + i, q_col + p)),
            pl.BlockSpec((seq, width), lambda b, p, i: (b, k_col + p)),
            pl.BlockSpec((seq, width), lambda b, p, i: (b, v_col + p)),
            _resident((2 * SB_BLOCK, 2 * SB_BLOCK)),
        ],
        out_specs=pl.BlockSpec((SB_BLOCK, width), lambda b, p, i: (b * nq + i, p)),
        out_shape=jax.ShapeDtypeStruct((bsz * seq, n_pairs * LANES), BF16),
        scratch_shapes=[pltpu.VMEM((gp, 2 * SB_BLOCK, LANES), BF16),
                        pltpu.VMEM((gp, 2 * SB_BLOCK, LANES), F32),
                        pltpu.VMEM((gp, 2 * SB_BLOCK, SB_BLOCK), F32)],
        compiler_params=_cparams("parallel", "parallel", "arbitrary"),
        name="sb_attention",
    )(proj, proj, proj, uu)


def _ssd_kernel(xbc_ref, z_ref, dt_ref, dtt_ref, convw_ref, convb_ref, dtb_ref, dtbt_ref,
                alog_ref, alogt_ref, dskip_ref, gnorm_ref, tri_ref, trit_ref,
                o_ref, state_ref, halo_ref, *, inner, n_heads):
    c = pl.program_id(1)
    cl = SSM_CHUNK
    gw = SSM_GROUPS * SSM_STATE
    hpg = n_heads // SSM_GROUPS

    @pl.when(c == 0)
    def _():
        state_ref[...] = jnp.zeros_like(state_ref)
        halo_ref[...] = jnp.zeros_like(halo_ref)

    xin = xbc_ref[...].astype(F32)
    halo = halo_ref[...]
    row8 = lax.broadcasted_iota(jnp.int32, halo.shape, 0)
    conv = xin * convw_ref[SSM_CONV - 1:SSM_CONV, :] + convb_ref[...]
    for k in range(1, SSM_CONV):
        cur = pltpu.roll(xin, k, 0)
        top = jnp.where(row8 < k, pltpu.roll(halo, k, 0), cur[:8])
        shifted = jnp.concatenate([top, cur[8:]], axis=0)
        conv = conv + shifted * convw_ref[SSM_CONV - 1 - k:SSM_CONV - k, :]
    halo_ref[...] = xin[cl - 8:, :]
    act = conv * _sigmoid(conv)
    xs = act[:, :inner]
    bm = act[:, inner:inner + gw]
    cm = act[:, inner + gw:]

    def softplus(v):
        return jnp.maximum(v, 0.0) + jnp.log(1.0 + jnp.exp(-jnp.abs(v)))

    dt_r = softplus(dt_ref[...] + dtb_ref[...])
    dt_t = softplus(dtt_ref[...] + dtbt_ref[...])
    adt_r = dt_r * (-jnp.exp(alog_ref[...]))
    adt_t = dt_t * (-jnp.exp(alogt_ref[...]))
    acs_r = sum(jnp.dot(tri_ref[...], p, preferred_element_type=F32) for p in _split3(adt_r))
    acs_t = sum(jnp.dot(p, trit_ref[...], preferred_element_type=F32) for p in _split3(adt_t))

    lane = lax.broadcasted_iota(jnp.int32, (cl, LANES), 1)
    first_head = lane < SSM_HEAD_DIM
    tl = lax.broadcasted_iota(jnp.int32, (cl, cl), 0)
    ts = lax.broadcasted_iota(jnp.int32, (cl, cl), 1)
    tril = ts <= tl

    y_pairs = []
    for g in range(SSM_GROUPS):
        bg = bm[:, g * SSM_STATE:(g + 1) * SSM_STATE]
        cg = cm[:, g * SSM_STATE:(g + 1) * SSM_STATE]
        cb = lax.dot_general(cg.astype(BF16), bg.astype(BF16), (((1,), (1,)), ((), ())),
                             preferred_element_type=F32)
        bgt = bg.T
        for pi in range(hpg // 2):
            h0 = g * hpg + 2 * pi
            lhs_y, lhs_s, cdec = [], [], []
            for h in (h0, h0 + 1):
                a_col = acs_r[:, h:h + 1]
                a_row = acs_t[h:h + 1, :]
                dt_row = dt_t[h:h + 1, :]
                a_end = acs_t[h:h + 1, cl - 1:cl]
                decay = jnp.exp(jnp.where(tril, a_col - a_row, -jnp.inf))
                m = cb * decay * dt_row
                c_scaled = cg * jnp.exp(a_col)
                lhs_y.append(jnp.concatenate([m.astype(BF16), c_scaled.astype(BF16)], axis=1))
                w_row = jnp.exp(a_end - a_row) * dt_row
                lhs_s.append((bgt * w_row).astype(BF16))
                cdec.append(jnp.exp(a_end))
            cols = slice(h0 * SSM_HEAD_DIM, (h0 + 2) * SSM_HEAD_DIM)
            x_pair = xs[:, cols].astype(BF16)
            s_prev = state_ref[:, cols]
            rhs = jnp.concatenate([x_pair, s_prev.astype(BF16)], axis=0)
            out = jnp.dot(jnp.concatenate(lhs_y, axis=0), rhs, preferred_element_type=F32)
            y_pairs.append(jnp.where(first_head, out[:cl], out[cl:]))
            st = jnp.dot(jnp.concatenate(lhs_s, axis=0), x_pair, preferred_element_type=F32)
            st = jnp.where(first_head, st[:SSM_STATE], st[SSM_STATE:])
            state_ref[:, cols] = s_prev * jnp.where(first_head, cdec[0], cdec[1]) + st

    y = jnp.concatenate(y_pairs, axis=1) + dskip_ref[...] * xs
    zf = z_ref[...].astype(F32)
    y = y * (zf * _sigmoid(zf))
    gsz = inner // SSM_GROUPS
    outs = []
    for g in range(SSM_GROUPS):
        yg = y[:, g * gsz:(g + 1) * gsz]
        outs.append(yg * lax.rsqrt(jnp.mean(yg * yg, axis=-1, keepdims=True) + RMS_EPS))
    o_ref[...] = (jnp.concatenate(outs, axis=1) * gnorm_ref[...]).astype(o_ref.dtype)


def _ssd(proj, dt, dtt, prm, bsz, seq, inner, n_heads, xbc_col, z_col):
    nc = seq // SSM_CHUNK
    cw = inner + 2 * SSM_GROUPS * SSM_STATE
    const = lambda shape: pl.BlockSpec(shape, lambda b, c: (0, 0))
    kern = functools.partial(_ssd_kernel, inner=inner, n_heads=n_heads)
    return pl.pallas_call(
        kern,
        grid=(bsz, nc),
        in_specs=[
            pl.BlockSpec((SSM_CHUNK, cw), lambda b, c: (b * nc + c, xbc_col)),
            pl.BlockSpec((SSM_CHUNK, inner), lambda b, c: (b * nc + c, z_col)),
            pl.BlockSpec((SSM_CHUNK, LANES), lambda b, c: (b * nc + c, 0)),
            pl.BlockSpec((n_heads, SSM_CHUNK), lambda b, c: (0, b * nc + c)),
            const((SSM_CONV, cw)),
            const((1, cw)),
            const((1, LANES)),
            const((n_heads, SSM_CHUNK)),
            const((1, LANES)),
            const((n_heads, SSM_CHUNK)),
            const((1, inner)),
            const((1, inner)),
            const((SSM_CHUNK, SSM_CHUNK)),
            const((SSM_CHUNK, SSM_CHUNK)),
        ],
        out_specs=pl.BlockSpec((SSM_CHUNK, inner), lambda b, c: (b * nc + c, 0)),
        out_shape=jax.ShapeDtypeStruct((bsz * seq, inner), BF16),
        scratch_shapes=[pltpu.VMEM((SSM_STATE, inner), F32), pltpu.VMEM((8, cw), F32)],
        compiler_params=_cparams("parallel", "arbitrary"),
        name="ssd",
    )(proj, proj, dt, dtt, prm["conv_w"], prm["conv_b"], prm["dt_bias_row"], prm["dt_bias_col"],
      prm["a_log_row"], prm["a_log_col"], prm["d_skip"], prm["g_norm"], prm["tri"], prm["trit"])


def _merge_kernel(x_ref, oa_ref, os_ref, ga_ref, gs_ref, wa_ref, ws_ref, wm_ref, g_ref, o_ref):
    a = jnp.dot(oa_ref[...], wa_ref[...], preferred_element_type=F32)
    s = jnp.dot(os_ref[...], ws_ref[...], preferred_element_type=F32)
    merged = _sigmoid(ga_ref[...].astype(F32)) * a + _sigmoid(gs_ref[...].astype(F32)) * s
    y = jnp.dot(merged.astype(BF16), wm_ref[...], preferred_element_type=F32)
    o_ref[...] = x_ref[...] + _rms(y, g_ref[...])


def _merge(x2, o_att, o_ssm, proj, wa, ws, wm, g, tm, ga_col, gs_col):
    t, d = x2.shape
    inner = o_ssm.shape[1]
    sbw = o_att.shape[1]
    return pl.pallas_call(
        _merge_kernel,
        grid=(t // tm,),
        in_specs=[
            pl.BlockSpec((tm, d), lambda i: (i, 0)),
            pl.BlockSpec((tm, sbw), lambda i: (i, 0)),
            pl.BlockSpec((tm, inner), lambda i: (i, 0)),
            pl.BlockSpec((tm, d), lambda i: (i, ga_col)),
            pl.BlockSpec((tm, d), lambda i: (i, gs_col)),
            _resident((sbw, d)),
            _resident((inner, d)),
            _resident((d, d)),
            _resident((1, d)),
        ],
        out_specs=pl.BlockSpec((tm, d), lambda i: (i, 0)),
        out_shape=jax.ShapeDtypeStruct((t, d), F32),
        compiler_params=_cparams("parallel"),
        name="merge",
    )(x2, o_att, o_ssm, proj, proj, wa, ws, wm, g)


def _xattn_kernel(x_ref, kv_ref, gpre_ref, wq_ref, wo_ref, gpost_ref, o_ref, *, d):
    x = x_ref[...]
    h = _rms(x, gpre_ref[...]).astype(BF16)
    hd = d // XA_HEADS
    q = (jnp.dot(h, wq_ref[...], preferred_element_type=F32) * (hd ** -0.5)).astype(BF16)
    outs = []
    for n in range(XA_HEADS):
        kh = kv_ref[:, n * hd:(n + 1) * hd]
        vh = kv_ref[:, d + n * hd:d + (n + 1) * hd]
        s = lax.dot_general(q[:, n * hd:(n + 1) * hd], kh, (((1,), (1,)), ((), ())),
                            preferred_element_type=F32)
        p = jnp.exp(s - jnp.max(s, axis=-1, keepdims=True))
        denom = jnp.sum(p, axis=-1, keepdims=True)
        o = jnp.dot(p.astype(BF16), vh, preferred_element_type=F32)
        outs.append((o / denom).astype(BF16))
    o = jnp.concatenate(outs, axis=1)
    y = jnp.dot(o, wo_ref[...], preferred_element_type=F32)
    o_ref[...] = x + _rms(y, gpost_ref[...])


def _xattn(x2, kv, gpre, wq, wo, gpost, tm, seq, mem_len):
    t, d = x2.shape
    per_b = seq // tm
    return pl.pallas_call(
        functools.partial(_xattn_kernel, d=d),
        grid=(t // tm,),
        in_specs=[
            pl.BlockSpec((tm, d), lambda i: (i, 0)),
            pl.BlockSpec((mem_len, 2 * d), lambda i: (i // per_b, 0)),
            _resident((1, d)),
            _resident((d, d)),
            _resident((d, d)),
            _resident((1, d)),
        ],
        out_specs=pl.BlockSpec((tm, d), lambda i: (i, 0)),
        out_shape=jax.ShapeDtypeStruct((t, d), F32),
        compiler_params=_cparams("parallel"),
        name="xattn",
    )(x2, kv, gpre, wq, wo, gpost)


def _ffn_kernel(x_ref, gpre_ref, wgu_ref, wd_ref, gpost_ref, o_ref, *, hidden, chunk):
    x = x_ref[...]
    h = _rms(x, gpre_ref[...]).astype(BF16)
    y = jnp.zeros(x.shape, F32)
    for c0 in range(0, hidden, chunk):
        gate = jnp.dot(h, wgu_ref[:, c0:c0 + chunk], preferred_element_type=F32)
        up = jnp.dot(h, wgu_ref[:, hidden + c0:hidden + c0 + chunk], preferred_element_type=F32)
        act = (gate * _sigmoid(gate) * up).astype(BF16)
        y = y + jnp.dot(act, wd_ref[c0:c0 + chunk, :], preferred_element_type=F32)
    o_ref[...] = x + _rms(y, gpost_ref[...])


def _ffn(x2, gpre, wgu, wd, gpost, tm):
    t, d = x2.shape
    hidden = wd.shape[0]
    chunk = hidden
    for cand in (512, 384, 256, 128):
        if hidden % cand == 0:
            chunk = cand
            break
    return pl.pallas_call(
        functools.partial(_ffn_kernel, hidden=hidden, chunk=chunk),
        grid=(t // tm,),
        in_specs=[
            pl.BlockSpec((tm, d), lambda i: (i, 0)),
            _resident((1, d)),
            _resident((d, 2 * hidden)),
            _resident((hidden, d)),
            _resident((1, d)),
        ],
        out_specs=pl.BlockSpec((tm, d), lambda i: (i, 0)),
        out_shape=jax.ShapeDtypeStruct((t, d), F32),
        compiler_params=_cparams("parallel"),
        name="ffn",
    )(x2, gpre, wgu, wd, gpost)


def _segment_layout(widths, tile):
    offs, cur = [], 0
    for w in widths:
        cur = -(-cur // w) * w
        offs.append(cur)
        cur += w
    return offs, -(-cur // tile) * tile


def kernel(x, mem, g_pre_mix, w_in, conv_w, conv_b, dt_bias, a_log, d_skip, g_ssm_norm, w_br_att,
           w_br_ssm, w_mix_out, g_post_mix, g_pre_xa, g_mem, w_xq, w_xkv, w_xo, g_post_xa,
           g_pre_ffn, w_gu, w_down, g_post_ffn):
    bsz, seq, d = x.shape
    depth = w_in.shape[0]
    mem_len = mem.shape[1]
    sbw = w_br_att.shape[1]
    inner = w_br_ssm.shape[1]
    n_heads = inner // SSM_HEAD_DIM
    gw = SSM_GROUPS * SSM_STATE
    cw = inner + 2 * gw
    t = bsz * seq
    assert seq % SB_BLOCK == 0 and seq % SSM_CHUNK == 0 and sbw % LANES == 0
    assert n_heads % (2 * SSM_GROUPS) == 0 and n_heads <= LANES and (d // XA_HEADS) % LANES == 0

    src = {}
    cur = 0
    for name, wd_ in (("q", sbw), ("k", sbw), ("v", sbw), ("z", inner), ("xbc", cw),
                      ("dt", n_heads), ("ga", d), ("gs", d)):
        src[name] = (cur, wd_)
        cur += wd_
    order = ("q", "k", "v", "xbc", "z", "ga", "gs")
    tn = _pick_tile(1 << 30, 1024)
    offs, n_total = _segment_layout([src[k][1] for k in order], tn)
    col = dict(zip(order, offs))

    tm_proj = _pick_tile(t, 1024)
    tm_row = _pick_tile(seq, 512)

    ti = lax.broadcasted_iota(jnp.int32, (SSM_CHUNK, SSM_CHUNK), 0)
    tj = lax.broadcasted_iota(jnp.int32, (SSM_CHUNK, SSM_CHUNK), 1)
    tri = (tj <= ti).astype(BF16)
    trit = (ti <= tj).astype(BF16)
    suffix = (ti >= tj).astype(BF16)
    half = jnp.concatenate([suffix, jnp.ones((SB_BLOCK, SB_BLOCK), BF16)], axis=1)
    uu = jnp.concatenate([half, half], axis=0)

    x2 = x.reshape(t, d)
    mem2 = mem.reshape(bsz * mem_len, d)
    row = lambda v: v.reshape(1, -1)
    for l in range(depth):
        w_full = w_in[l]
        w_perm = jnp.zeros((d, n_total), BF16)
        for k in order:
            s0, wd_ = src[k]
            w_perm = lax.dynamic_update_slice(w_perm, w_full[:, s0:s0 + wd_].astype(BF16), (0, col[k]))
        s0, wd_ = src["dt"]
        w_dt = w_full[:, s0:s0 + wd_].astype(BF16)
        w_dt_pad = jnp.zeros((d, LANES), BF16).at[:, :n_heads].set(w_dt)
        proj, dt, dtt = _inproj(x2, row(g_pre_mix[l]), w_perm, w_dt_pad, w_dt.T, tm_proj, tn)

        n_pairs = sbw // LANES
        gp = SB_PAIRS_PER_STEP if n_pairs % SB_PAIRS_PER_STEP == 0 else 1
        o_att = _sb_attention(proj, uu, bsz, seq, n_pairs, gp, col["q"] // (gp * LANES),
                              col["k"] // (gp * LANES), col["v"] // (gp * LANES))
        pad_row = lambda v: jnp.zeros((1, LANES), F32).at[0, :n_heads].set(v)
        prm = {
            "conv_w": conv_w[l], "conv_b": row(conv_b[l]),
            "dt_bias_row": pad_row(dt_bias[l]),
            "dt_bias_col": jnp.broadcast_to(dt_bias[l][:, None], (n_heads, SSM_CHUNK)),
            "a_log_row": pad_row(a_log[l]),
            "a_log_col": jnp.broadcast_to(a_log[l][:, None], (n_heads, SSM_CHUNK)),
            "d_skip": row(jnp.repeat(d_skip[l], SSM_HEAD_DIM)),
            "g_norm": row(g_ssm_norm[l]),
            "tri": tri, "trit": trit,
        }
        o_ssm = _ssd(proj, dt, dtt, prm, bsz, seq, inner, n_heads, col["xbc"] // cw, col["z"] // inner)
        x2 = _merge(x2, o_att, o_ssm, proj, w_br_att[l].astype(BF16), w_br_ssm[l].astype(BF16),
                    w_mix_out[l].astype(BF16), row(g_post_mix[l]), tm_row,
                    col["ga"] // d, col["gs"] // d)

        kv = _norm_matmul(mem2, row(g_mem[l]), w_xkv[l].astype(BF16), _pick_tile(bsz * mem_len, 512))
        x2 = _xattn(x2, kv, row(g_pre_xa[l]), w_xq[l].astype(BF16), w_xo[l].astype(BF16),
                    row(g_post_xa[l]), tm_row, seq, mem_len)

        x2 = _ffn(x2, row(g_pre_ffn[l]), w_gu[l].astype(BF16), w_down[l].astype(BF16),
                  row(g_post_ffn[l]), tm_row)
    return x2.reshape(bsz, seq, d)
```

```python
import functools

import jax
import jax.numpy as jnp
from jax import lax
from jax.experimental import pallas as pl
from jax.experimental.pallas import tpu as pltpu

F32 = jnp.float32
BF16 = jnp.bfloat16

RMS_EPS = 1e-6
LOG2E = 1.4426950408889634
SB_HEAD_DIM = 64
SB_BLOCK = 128
SB_PAIRS_PER_STEP = 8
SB_DEAD_LOG = -120.0
SSM_HEAD_DIM = 64
SSM_GROUPS = 4
SSM_STATE = 128
SSM_CONV = 4
SSM_CHUNK = 128
CONV_COLS = 512
XA_HEADS = 4
LANES = 128
VMEM_LIMIT_BYTES = 56 * 1024 * 1024


def _cparams(*sem):
    return pltpu.CompilerParams(dimension_semantics=sem, vmem_limit_bytes=VMEM_LIMIT_BYTES)


def _pick_tile(n, want):
    t = min(n, want)
    while n % t:
        t //= 2
    return t


def _resident(shape):
    return pl.BlockSpec(shape, lambda *_: (0,) * len(shape), pipeline_mode=pl.Buffered(1))


def _rms(x, g):
    ms = jnp.mean(x * x, axis=-1, keepdims=True)
    return x * lax.rsqrt(ms + RMS_EPS) * g


def _sigmoid(x):
    return 1.0 / (1.0 + jnp.exp2(x * (-LOG2E)))


def _split3(x):
    hi = x.astype(BF16)
    r = x - hi.astype(F32)
    mid = r.astype(BF16)
    lo = (r - mid.astype(F32)).astype(BF16)
    return hi, mid, lo


def _inproj_kernel(x_ref, g_ref, w_ref, wdt_ref, wdtt_ref, o_ref, dt_ref, dtt_ref, h_ref):
    @pl.when(pl.program_id(1) == 0)
    def _():
        h = _rms(x_ref[...], g_ref[...]).astype(BF16)
        h_ref[...] = h
        dt_ref[...] = jnp.dot(h, wdt_ref[...], preferred_element_type=F32)
        dtt_ref[...] = lax.dot_general(wdtt_ref[...], h, (((1,), (1,)), ((), ())),
                                       preferred_element_type=F32)

    o_ref[...] = jnp.dot(h_ref[...], w_ref[...], preferred_element_type=F32).astype(o_ref.dtype)


def _inproj(x2, g, w, wdt, wdtt, tm, tn):
    t, d = x2.shape
    n = w.shape[1]
    nh = wdtt.shape[0]
    return pl.pallas_call(
        _inproj_kernel,
        grid=(t // tm, n // tn),
        in_specs=[
            pl.BlockSpec((tm, d), lambda i, j: (i, 0)),
            pl.BlockSpec((1, d), lambda i, j: (0, 0)),
            pl.BlockSpec((d, tn), lambda i, j: (0, j)),
            pl.BlockSpec((d, LANES), lambda i, j: (0, 0)),
            pl.BlockSpec((nh, d), lambda i, j: (0, 0)),
        ],
        out_specs=[
            pl.BlockSpec((tm, tn), lambda i, j: (i, j)),
            pl.BlockSpec((tm, LANES), lambda i, j: (i, 0)),
            pl.BlockSpec((nh, tm), lambda i, j: (0, i)),
        ],
        out_shape=[
            jax.ShapeDtypeStruct((t, n), BF16),
            jax.ShapeDtypeStruct((t, LANES), F32),
            jax.ShapeDtypeStruct((nh, t), F32),
        ],
        scratch_shapes=[pltpu.VMEM((tm, d), BF16)],
        compiler_params=_cparams("parallel", "arbitrary"),
        name="inproj",
    )(x2, g, w, wdt, wdtt)


def _norm_matmul_kernel(x_ref, g_ref, w_ref, o_ref):
    h = _rms(x_ref[...], g_ref[...]).astype(BF16)
    o_ref[...] = jnp.dot(h, w_ref[...], preferred_element_type=F32).astype(o_ref.dtype)


def _norm_matmul(x2, g, w, tm):
    t, d = x2.shape
    n = w.shape[1]
    return pl.pallas_call(
        _norm_matmul_kernel,
        grid=(t // tm,),
        in_specs=[
            pl.BlockSpec((tm, d), lambda i: (i, 0)),
            pl.BlockSpec((1, d), lambda i: (0, 0)),
            _resident((d, n)),
        ],
        out_specs=pl.BlockSpec((tm, n), lambda i: (i, 0)),
        out_shape=jax.ShapeDtypeStruct((t, n), BF16),
        compiler_params=_cparams("parallel"),
        name="mem_kv",
    )(x2, g, w)


def _sb_kernel(q_ref, k_ref, v_ref, uu_ref, o_ref, qs_ref, acc_ref, run_ref, *, n_pairs):
    i = pl.program_id(2)
    blk = SB_BLOCK
    lane = lax.broadcasted_iota(jnp.int32, (blk, LANES), 1)
    first_head = lane < SB_HEAD_DIM
    row = lax.broadcasted_iota(jnp.int32, (2 * blk, blk), 0)
    col = lax.broadcasted_iota(jnp.int32, (2 * blk, blk), 1)
    causal = col < jnp.where(row >= blk, row - blk, row)

    for p in range(n_pairs):
        q = q_ref[:, p * LANES:(p + 1) * LANES] * (-(SB_HEAD_DIM ** -0.5))
        zero = jnp.zeros_like(q)
        qs_ref[p] = jnp.concatenate([jnp.where(first_head, q, zero), jnp.where(first_head, zero, q)],
                                    axis=0)

    def blocks(j, diag):
        start = pl.multiple_of(j * blk, blk)
        pairs = range(n_pairs)
        zns = [lax.dot_general(qs_ref[p], k_ref[pl.ds(start, blk), p * LANES:(p + 1) * LANES],
                               (((1,), (1,)), ((), ())), preferred_element_type=F32) for p in pairs]
        cs = []
        for p in pairs:
            zn = zns[p]
            log_1mb = jnp.minimum(zn, 0.0) - jnp.log(1.0 + jnp.exp2(jnp.abs(zn) * (-LOG2E)))
            if diag:
                log_1mb = jnp.where(causal, log_1mb, 0.0)
            hi = log_1mb.astype(BF16)
            lo = (log_1mb - hi.astype(F32)).astype(BF16)
            cs.append(jnp.dot(jnp.concatenate([hi, lo], axis=1), uu_ref[...],
                              preferred_element_type=F32))
        ws = []
        run_max = None
        for p in pairs:
            c = cs[p]
            if diag:
                tail = c[:, :blk]
                run = c[:, blk:]
            else:
                run = run_ref[p]
                tail = c[:, :blk] + run
                run = run + c[:, blk:]
            run_ref[p] = run
            run_max = run if run_max is None else jnp.maximum(run_max, run)
            w = jnp.exp2((tail - zns[p]) * LOG2E)
            if diag:
                w = jnp.where(causal, w, 0.0)
            ws.append(w.astype(BF16))
        for p in pairs:
            pv = jnp.dot(ws[p], v_ref[pl.ds(start, blk), p * LANES:(p + 1) * LANES],
                         preferred_element_type=F32)
            if diag:
                acc_ref[p] = pv
            else:
                acc_ref[p] += pv
        return (jnp.max(run_max) > SB_DEAD_LOG).astype(jnp.int32)

    live = blocks(i, True)

    def body(carry):
        jj, _ = carry
        return jj + 1, blocks(i - 1 - jj, False)

    lax.while_loop(lambda carry: jnp.logical_and(carry[0] < i, carry[1] > 0), body,
                   (jnp.int32(0), live))
    for p in range(n_pairs):
        acc = acc_ref[p]
        o_ref[:, p * LANES:(p + 1) * LANES] = jnp.where(first_head, acc[:blk], acc[blk:]).astype(o_ref.dtype)


def _sb_attention(proj, uu, bsz, seq, n_pairs, pairs_per_step, q_col, k_col, v_col):
    nq = seq // SB_BLOCK
    gp = pairs_per_step
    width = gp * LANES
    return pl.pallas_call(
        functools.partial(_sb_kernel, n_pairs=gp),
        grid=(bsz, n_pairs // gp, nq),
        in_specs=[
            pl.BlockSpec((SB_BLOCK, width), lambda b, p, i: (b * nq + i, q_col + p)),
            pl.BlockSpec((seq, width), lambda b, p, i: (b, k_col + p)),
            pl.BlockSpec((seq, width), lambda b, p, i: (b, v_col + p)),
            _resident((2 * SB_BLOCK, 2 * SB_BLOCK)),
        ],
        out_specs=pl.BlockSpec((SB_BLOCK, width), lambda b, p, i: (b * nq + i, p)),
        out_shape=jax.ShapeDtypeStruct((bsz * seq, n_pairs * LANES), BF16),
        scratch_shapes=[pltpu.VMEM((gp, 2 * SB_BLOCK, LANES), BF16),
                        pltpu.VMEM((gp, 2 * SB_BLOCK, LANES), F32),
                        pltpu.VMEM((gp, 2 * SB_BLOCK, SB_BLOCK), F32)],
        compiler_params=_cparams("parallel", "parallel", "arbitrary"),
        name="sb_attention",
    )(proj, proj, proj, uu)


def _ssd_kernel(xbc_ref, z_ref, dt_ref, dtt_ref, convw_ref, convb_ref, dtb_ref, dtbt_ref,
                alog_ref, alogt_ref, dskip_ref, gnorm_ref, tri_ref, trit_ref, shift_ref,
                o_ref, state_ref, prev_ref, *, inner, n_heads):
    c = pl.program_id(1)
    cl = SSM_CHUNK
    gw = SSM_GROUPS * SSM_STATE
    hpg = n_heads // SSM_GROUPS

    @pl.when(c == 0)
    def _():
        state_ref[...] = jnp.zeros_like(state_ref)
        prev_ref[...] = jnp.zeros_like(prev_ref)

    cur = xbc_ref[...]
    both = jnp.concatenate([prev_ref[...], cur], axis=0)
    prev_ref[...] = cur
    conv_cols = []
    for c0 in range(0, cur.shape[1], CONV_COLS):
        cols = slice(c0, c0 + CONV_COLS)
        delayed = jnp.dot(shift_ref[...], both[:, cols], preferred_element_type=F32)
        acc = cur[:, cols].astype(F32) * convw_ref[SSM_CONV - 1:SSM_CONV, cols] + convb_ref[:, cols]
        for k in range(1, SSM_CONV):
            acc = acc + delayed[(k - 1) * cl:k * cl] * convw_ref[SSM_CONV - 1 - k:SSM_CONV - k, cols]
        conv_cols.append(acc)
    conv = jnp.concatenate(conv_cols, axis=1)
    act = conv * _sigmoid(conv)
    xs = act[:, :inner]
    bm = act[:, inner:inner + gw]
    cm = act[:, inner + gw:]

    def softplus(v):
        return jnp.maximum(v, 0.0) + jnp.log(1.0 + jnp.exp2(jnp.abs(v) * (-LOG2E)))

    dt_r = softplus(dt_ref[...] + dtb_ref[...])
    dt_t = softplus(dtt_ref[...] + dtbt_ref[...])
    adt_r = dt_r * (-jnp.exp(alog_ref[...]))
    adt_t = dt_t * (-jnp.exp(alogt_ref[...]))
    acs_r = sum(jnp.dot(tri_ref[...], p, preferred_element_type=F32) for p in _split3(adt_r)) * LOG2E
    acs_t = sum(jnp.dot(p, trit_ref[...], preferred_element_type=F32) for p in _split3(adt_t)) * LOG2E
    src_t = acs_t - jnp.log(dt_t) * LOG2E

    lane = lax.broadcasted_iota(jnp.int32, (cl, LANES), 1)
    first_head = lane < SSM_HEAD_DIM
    tl = lax.broadcasted_iota(jnp.int32, (cl, cl), 0)
    ts = lax.broadcasted_iota(jnp.int32, (cl, cl), 1)
    tril = ts <= tl

    y_pairs = []
    for g in range(SSM_GROUPS):
        bg = bm[:, g * SSM_STATE:(g + 1) * SSM_STATE]
        cg = cm[:, g * SSM_STATE:(g + 1) * SSM_STATE]
        cb = lax.dot_general(cg.astype(BF16), bg.astype(BF16), (((1,), (1,)), ((), ())),
                             preferred_element_type=F32)
        bgt = bg.T
        for pi in range(hpg // 2):
            h0 = g * hpg + 2 * pi
            lhs_y, lhs_s, cdec = [], [], []
            for h in (h0, h0 + 1):
                a_col = acs_r[:, h:h + 1]
                s_row = src_t[h:h + 1, :]
                a_end = acs_t[h:h + 1, cl - 1:cl]
                m = cb * jnp.exp2(jnp.where(tril, a_col - s_row, -jnp.inf))
                c_scaled = cg * jnp.exp2(a_col)
                lhs_y.append(jnp.concatenate([m.astype(BF16), c_scaled.astype(BF16)], axis=1))
                w_row = jnp.exp2(a_end - s_row)
                lhs_s.append((bgt * w_row).astype(BF16))
                cdec.append(jnp.exp2(a_end))
            cols = slice(h0 * SSM_HEAD_DIM, (h0 + 2) * SSM_HEAD_DIM)
            x_pair = xs[:, cols].astype(BF16)
            s_prev = state_ref[:, cols]
            rhs = jnp.concatenate([x_pair, s_prev.astype(BF16)], axis=0)
            out = jnp.dot(jnp.concatenate(lhs_y, axis=0), rhs, preferred_element_type=F32)
            y_pairs.append(jnp.where(first_head, out[:cl], out[cl:]))
            st = jnp.dot(jnp.concatenate(lhs_s, axis=0), x_pair, preferred_element_type=F32)
            st = jnp.where(first_head, st[:SSM_STATE], st[SSM_STATE:])
            state_ref[:, cols] = s_prev * jnp.where(first_head, cdec[0], cdec[1]) + st

    y = jnp.concatenate(y_pairs, axis=1) + dskip_ref[...] * xs
    zf = z_ref[...].astype(F32)
    y = y * (zf * _sigmoid(zf))
    gsz = inner // SSM_GROUPS
    outs = []
    for g in range(SSM_GROUPS):
        yg = y[:, g * gsz:(g + 1) * gsz]
        outs.append(yg * lax.rsqrt(jnp.mean(yg * yg, axis=-1, keepdims=True) + RMS_EPS))
    o_ref[...] = (jnp.concatenate(outs, axis=1) * gnorm_ref[...]).astype(o_ref.dtype)


def _ssd(proj, dt, dtt, prm, bsz, seq, inner, n_heads, xbc_col, z_col):
    nc = seq // SSM_CHUNK
    cw = inner + 2 * SSM_GROUPS * SSM_STATE
    const = lambda shape: pl.BlockSpec(shape, lambda b, c: (0, 0))
    kern = functools.partial(_ssd_kernel, inner=inner, n_heads=n_heads)
    return pl.pallas_call(
        kern,
        grid=(bsz, nc),
        in_specs=[
            pl.BlockSpec((SSM_CHUNK, cw), lambda b, c: (b * nc + c, xbc_col)),
            pl.BlockSpec((SSM_CHUNK, inner), lambda b, c: (b * nc + c, z_col)),
            pl.BlockSpec((SSM_CHUNK, LANES), lambda b, c: (b * nc + c, 0)),
            pl.BlockSpec((n_heads, SSM_CHUNK), lambda b, c: (0, b * nc + c)),
            const((SSM_CONV, cw)),
            const((1, cw)),
            const((1, LANES)),
            const((n_heads, SSM_CHUNK)),
            const((1, LANES)),
            const((n_heads, SSM_CHUNK)),
            const((1, inner)),
            const((1, inner)),
            const((SSM_CHUNK, SSM_CHUNK)),
            const((SSM_CHUNK, SSM_CHUNK)),
            const(((SSM_CONV - 1) * SSM_CHUNK, 2 * SSM_CHUNK)),
        ],
        out_specs=pl.BlockSpec((SSM_CHUNK, inner), lambda b, c: (b * nc + c, 0)),
        out_shape=jax.ShapeDtypeStruct((bsz * seq, inner), BF16),
        scratch_shapes=[pltpu.VMEM((SSM_STATE, inner), F32), pltpu.VMEM((SSM_CHUNK, cw), BF16)],
        compiler_params=_cparams("parallel", "arbitrary"),
        name="ssd",
    )(proj, proj, dt, dtt, prm["conv_w"], prm["conv_b"], prm["dt_bias_row"], prm["dt_bias_col"],
      prm["a_log_row"], prm["a_log_col"], prm["d_skip"], prm["g_norm"], prm["tri"], prm["trit"],
      prm["shift"])


def _merge_kernel(x_ref, oa_ref, os_ref, ga_ref, gs_ref, wa_ref, ws_ref, wm_ref, g_ref, o_ref):
    a = jnp.dot(oa_ref[...], wa_ref[...], preferred_element_type=F32)
    s = jnp.dot(os_ref[...], ws_ref[...], preferred_element_type=F32)
    merged = _sigmoid(ga_ref[...].astype(F32)) * a + _sigmoid(gs_ref[...].astype(F32)) * s
    y = jnp.dot(merged.astype(BF16), wm_ref[...], preferred_element_type=F32)
    o_ref[...] = x_ref[...] + _rms(y, g_ref[...])


def _merge(x2, o_att, o_ssm, proj, wa, ws, wm, g, tm, ga_col, gs_col):
    t, d = x2.shape
    inner = o_ssm.shape[1]
    sbw = o_att.shape[1]
    return pl.pallas_call(
        _merge_kernel,
        grid=(t // tm,),
        in_specs=[
            pl.BlockSpec((tm, d), lambda i: (i, 0)),
            pl.BlockSpec((tm, sbw), lambda i: (i, 0)),
            pl.BlockSpec((tm, inner), lambda i: (i, 0)),
            pl.BlockSpec((tm, d), lambda i: (i, ga_col)),
            pl.BlockSpec((tm, d), lambda i: (i, gs_col)),
            _resident((sbw, d)),
            _resident((inner, d)),
            _resident((d, d)),
            _resident((1, d)),
        ],
        out_specs=pl.BlockSpec((tm, d), lambda i: (i, 0)),
        out_shape=jax.ShapeDtypeStruct((t, d), F32),
        compiler_params=_cparams("parallel"),
        name="merge",
    )(x2, o_att, o_ssm, proj, proj, wa, ws, wm, g)


def _xattn_kernel(x_ref, kv_ref, gpre_ref, wq_ref, wo_ref, gpost_ref, o_ref, *, d):
    x = x_ref[...]
    h = _rms(x, gpre_ref[...]).astype(BF16)
    hd = d // XA_HEADS
    q = (jnp.dot(h, wq_ref[...], preferred_element_type=F32) * (hd ** -0.5)).astype(BF16)
    outs = []
    for n in range(XA_HEADS):
        kh = kv_ref[:, n * hd:(n + 1) * hd]
        vh = kv_ref[:, d + n * hd:d + (n + 1) * hd]
        s = lax.dot_general(q[:, n * hd:(n + 1) * hd], kh, (((1,), (1,)), ((), ())),
                            preferred_element_type=F32)
        p = jnp.exp(s - jnp.max(s, axis=-1, keepdims=True))
        denom = jnp.sum(p, axis=-1, keepdims=True)
        o = jnp.dot(p.astype(BF16), vh, preferred_element_type=F32)
        outs.append((o / denom).astype(BF16))
    o = jnp.concatenate(outs, axis=1)
    y = jnp.dot(o, wo_ref[...], preferred_element_type=F32)
    o_ref[...] = x + _rms(y, gpost_ref[...])


def _xattn(x2, kv, gpre, wq, wo, gpost, tm, seq, mem_len):
    t, d = x2.shape
    per_b = seq // tm
    return pl.pallas_call(
        functools.partial(_xattn_kernel, d=d),
        grid=(t // tm,),
        in_specs=[
            pl.BlockSpec((tm, d), lambda i: (i, 0)),
            pl.BlockSpec((mem_len, 2 * d), lambda i: (i // per_b, 0)),
            _resident((1, d)),
            _resident((d, d)),
            _resident((d, d)),
            _resident((1, d)),
        ],
        out_specs=pl.BlockSpec((tm, d), lambda i: (i, 0)),
        out_shape=jax.ShapeDtypeStruct((t, d), F32),
        compiler_params=_cparams("parallel"),
        name="xattn",
    )(x2, kv, gpre, wq, wo, gpost)


def _ffn_kernel(x_ref, gpre_ref, wgu_ref, wd_ref, gpost_ref, o_ref, *, hidden, chunk):
    x = x_ref[...]
    h = _rms(x, gpre_ref[...]).astype(BF16)
    y = jnp.zeros(x.shape, F32)
    for c0 in range(0, hidden, chunk):
        gate = jnp.dot(h, wgu_ref[:, c0:c0 + chunk], preferred_element_type=F32)
        up = jnp.dot(h, wgu_ref[:, hidden + c0:hidden + c0 + chunk], preferred_element_type=F32)
        act = (gate * _sigmoid(gate) * up).astype(BF16)
        y = y + jnp.dot(act, wd_ref[c0:c0 + chunk, :], preferred_element_type=F32)
    o_ref[...] = x + _rms(y, gpost_ref[...])


def _ffn(x2, gpre, wgu, wd, gpost, tm):
    t, d = x2.shape
    hidden = wd.shape[0]
    chunk = hidden
    for cand in (512, 384, 256, 128):
        if hidden % cand == 0:
            chunk = cand
            break
    return pl.pallas_call(
        functools.partial(_ffn_kernel, hidden=hidden, chunk=chunk),
        grid=(t // tm,),
        in_specs=[
            pl.BlockSpec((tm, d), lambda i: (i, 0)),
            _resident((1, d)),
            _resident((d, 2 * hidden)),
            _resident((hidden, d)),
            _resident((1, d)),
        ],
        out_specs=pl.BlockSpec((tm, d), lambda i: (i, 0)),
        out_shape=jax.ShapeDtypeStruct((t, d), F32),
        compiler_params=_cparams("parallel"),
        name="ffn",
    )(x2, gpre, wgu, wd, gpost)


def _segment_layout(widths, tile):
    offs, cur = [], 0
    for w in widths:
        cur = -(-cur // w) * w
        offs.append(cur)
        cur += w
    return offs, -(-cur // tile) * tile


def kernel(x, mem, g_pre_mix, w_in, conv_w, conv_b, dt_bias, a_log, d_skip, g_ssm_norm, w_br_att,
           w_br_ssm, w_mix_out, g_post_mix, g_pre_xa, g_mem, w_xq, w_xkv, w_xo, g_post_xa,
           g_pre_ffn, w_gu, w_down, g_post_ffn):
    bsz, seq, d = x.shape
    depth = w_in.shape[0]
    mem_len = mem.shape[1]
    sbw = w_br_att.shape[1]
    inner = w_br_ssm.shape[1]
    n_heads = inner // SSM_HEAD_DIM
    gw = SSM_GROUPS * SSM_STATE
    cw = inner + 2 * gw
    t = bsz * seq
    assert seq % SB_BLOCK == 0 and seq % SSM_CHUNK == 0 and sbw % LANES == 0
    assert n_heads % (2 * SSM_GROUPS) == 0 and n_heads <= LANES and (d // XA_HEADS) % LANES == 0

    src = {}
    cur = 0
    for name, wd_ in (("q", sbw), ("k", sbw), ("v", sbw), ("z", inner), ("xbc", cw),
                      ("dt", n_heads), ("ga", d), ("gs", d)):
        src[name] = (cur, wd_)
        cur += wd_
    order = ("q", "k", "v", "xbc", "z", "ga", "gs")
    tn = _pick_tile(1 << 30, 1024)
    offs, n_total = _segment_layout([src[k][1] for k in order], tn)
    col = dict(zip(order, offs))

    tm_proj = _pick_tile(t, 2048)
    tm_row = _pick_tile(seq, 512)

    ti = lax.broadcasted_iota(jnp.int32, (SSM_CHUNK, SSM_CHUNK), 0)
    tj = lax.broadcasted_iota(jnp.int32, (SSM_CHUNK, SSM_CHUNK), 1)
    tri = (tj <= ti).astype(BF16)
    trit = (ti <= tj).astype(BF16)
    suffix = (ti >= tj).astype(BF16)
    half = jnp.concatenate([suffix, jnp.ones((SB_BLOCK, SB_BLOCK), BF16)], axis=1)
    uu = jnp.concatenate([half, half], axis=0)
    si = lax.broadcasted_iota(jnp.int32, (SSM_CHUNK, 2 * SSM_CHUNK), 0)
    sj = lax.broadcasted_iota(jnp.int32, (SSM_CHUNK, 2 * SSM_CHUNK), 1)
    shift = jnp.concatenate([(sj == si + SSM_CHUNK - k).astype(BF16) for k in range(1, SSM_CONV)],
                            axis=0)

    x2 = x.reshape(t, d)
    mem2 = mem.reshape(bsz * mem_len, d)
    row = lambda v: v.reshape(1, -1)
    for l in range(depth):
        w_full = w_in[l]
        w_perm = jnp.zeros((d, n_total), BF16)
        for k in order:
            s0, wd_ = src[k]
            w_perm = lax.dynamic_update_slice(w_perm, w_full[:, s0:s0 + wd_].astype(BF16), (0, col[k]))
        s0, wd_ = src["dt"]
        w_dt = w_full[:, s0:s0 + wd_].astype(BF16)
        w_dt_pad = jnp.zeros((d, LANES), BF16).at[:, :n_heads].set(w_dt)
        proj, dt, dtt = _inproj(x2, row(g_pre_mix[l]), w_perm, w_dt_pad, w_dt.T, tm_proj, tn)

        n_pairs = sbw // LANES
        gp = SB_PAIRS_PER_STEP if n_pairs % SB_PAIRS_PER_STEP == 0 else 1
        o_att = _sb_attention(proj, uu, bsz, seq, n_pairs, gp, col["q"] // (gp * LANES),
                              col["k"] // (gp * LANES), col["v"] // (gp * LANES))
        pad_row = lambda v: jnp.zeros((1, LANES), F32).at[0, :n_heads].set(v)
        prm = {
            "conv_w": conv_w[l], "conv_b": row(conv_b[l]),
            "dt_bias_row": pad_row(dt_bias[l]),
            "dt_bias_col": jnp.broadcast_to(dt_bias[l][:, None], (n_heads, SSM_CHUNK)),
            "a_log_row": pad_row(a_log[l]),
            "a_log_col": jnp.broadcast_to(a_log[l][:, None], (n_heads, SSM_CHUNK)),
            "d_skip": row(jnp.repeat(d_skip[l], SSM_HEAD_DIM)),
            "g_norm": row(g_ssm_norm[l]),
            "tri": tri, "trit": trit, "shift": shift,
        }
        o_ssm = _ssd(proj, dt, dtt, prm, bsz, seq, inner, n_heads, col["xbc"] // cw, col["z"] // inner)
        x2 = _merge(x2, o_att, o_ssm, proj, w_br_att[l].astype(BF16), w_br_ssm[l].astype(BF16),
                    w_mix_out[l].astype(BF16), row(g_post_mix[l]), tm_row,
                    col["ga"] // d, col["gs"] // d)

        kv = _norm_matmul(mem2, row(g_mem[l]), w_xkv[l].astype(BF16), _pick_tile(bsz * mem_len, 512))
        x2 = _xattn(x2, kv, row(g_pre_xa[l]), w_xq[l].astype(BF16), w_xo[l].astype(BF16),
                    row(g_post_xa[l]), tm_row, seq, mem_len)

        x2 = _ffn(x2, row(g_pre_ffn[l]), w_gu[l].astype(BF16), w_down[l].astype(BF16),
                  row(g_post_ffn[l]), tm_row)
    return x2.reshape(bsz, seq, d)
```

```python
import functools

import jax
import jax.numpy as jnp
from jax import lax
from jax.experimental import pallas as pl
from jax.experimental.pallas import tpu as pltpu

F32 = jnp.float32
BF16 = jnp.bfloat16

RMS_EPS = 1e-6
LOG2E = 1.4426950408889634
SB_HEAD_DIM = 64
SB_BLOCK = 128
SB_PAIRS_PER_STEP = 8
SB_DEAD_LOG = -120.0
SSM_HEAD_DIM = 64
SSM_GROUPS = 4
SSM_STATE = 128
SSM_CONV = 4
SSM_CHUNK = 128
XA_HEADS = 4
LANES = 128
VMEM_LIMIT_BYTES = 56 * 1024 * 1024


def _cparams(*sem):
    return pltpu.CompilerParams(dimension_semantics=sem, vmem_limit_bytes=VMEM_LIMIT_BYTES)


def _pick_tile(n, want):
    t = min(n, want)
    while n % t:
        t //= 2
    return t


def _resident(shape):
    return pl.BlockSpec(shape, lambda *_: (0,) * len(shape), pipeline_mode=pl.Buffered(1))


def _layer_resident(shape, layer):
    return pl.BlockSpec((None,) + tuple(shape), lambda *_: (layer,) + (0,) * len(shape),
                        pipeline_mode=pl.Buffered(1))


def _rms(x, g):
    ms = jnp.mean(x * x, axis=-1, keepdims=True)
    return x * lax.rsqrt(ms + RMS_EPS) * g


def _sigmoid(x):
    return 1.0 / (1.0 + jnp.exp2(x * (-LOG2E)))


def _split3(x):
    hi = x.astype(BF16)
    r = x - hi.astype(F32)
    mid = r.astype(BF16)
    lo = (r - mid.astype(F32)).astype(BF16)
    return hi, mid, lo


def _inproj_kernel(x_ref, g_ref, w_ref, wdt_ref, wdtt_ref, o_ref, dt_ref, dtt_ref, h_ref):
    @pl.when(pl.program_id(1) == 0)
    def _():
        h = _rms(x_ref[...], g_ref[...]).astype(BF16)
        h_ref[...] = h
        dt_ref[...] = jnp.dot(h, wdt_ref[...], preferred_element_type=F32)
        dtt_ref[...] = lax.dot_general(wdtt_ref[...], h, (((1,), (1,)), ((), ())),
                                       preferred_element_type=F32)

    o_ref[...] = jnp.dot(h_ref[...], w_ref[...], preferred_element_type=F32).astype(o_ref.dtype)


def _inproj(x2, g, w, wdt, wdtt, layer, tm, tn):
    t, d = x2.shape
    n = w.shape[2]
    nh = wdtt.shape[1]
    return pl.pallas_call(
        _inproj_kernel,
        grid=(t // tm, n // tn),
        in_specs=[
            pl.BlockSpec((tm, d), lambda i, j: (i, 0)),
            pl.BlockSpec((1, d), lambda i, j: (0, 0)),
            pl.BlockSpec((None, d, tn), lambda i, j: (layer, 0, j)),
            pl.BlockSpec((None, d, LANES), lambda i, j: (layer, 0, 0)),
            pl.BlockSpec((None, nh, d), lambda i, j: (layer, 0, 0)),
        ],
        out_specs=[
            pl.BlockSpec((tm, tn), lambda i, j: (i, j)),
            pl.BlockSpec((tm, LANES), lambda i, j: (i, 0)),
            pl.BlockSpec((nh, tm), lambda i, j: (0, i)),
        ],
        out_shape=[
            jax.ShapeDtypeStruct((t, n), BF16),
            jax.ShapeDtypeStruct((t, LANES), F32),
            jax.ShapeDtypeStruct((nh, t), F32),
        ],
        scratch_shapes=[pltpu.VMEM((tm, d), BF16)],
        compiler_params=_cparams("parallel", "arbitrary"),
        name="inproj",
    )(x2, g, w, wdt, wdtt)


def _norm_matmul_kernel(x_ref, g_ref, w_ref, o_ref):
    h = _rms(x_ref[...], g_ref[...]).astype(BF16)
    o_ref[...] = jnp.dot(h, w_ref[...], preferred_element_type=F32).astype(o_ref.dtype)


def _norm_matmul(x2, g, w, layer, tm):
    t, d = x2.shape
    n = w.shape[2]
    return pl.pallas_call(
        _norm_matmul_kernel,
        grid=(t // tm,),
        in_specs=[
            pl.BlockSpec((tm, d), lambda i: (i, 0)),
            pl.BlockSpec((1, d), lambda i: (0, 0)),
            _layer_resident((d, n), layer),
        ],
        out_specs=pl.BlockSpec((tm, n), lambda i: (i, 0)),
        out_shape=jax.ShapeDtypeStruct((t, n), BF16),
        compiler_params=_cparams("parallel"),
        name="mem_kv",
    )(x2, g, w)


def _sb_kernel(q_ref, k_ref, v_ref, uu_ref, o_ref, qs_ref, acc_ref, run_ref, *, n_pairs):
    i = pl.program_id(2)
    blk = SB_BLOCK
    lane = lax.broadcasted_iota(jnp.int32, (blk, LANES), 1)
    first_head = lane < SB_HEAD_DIM
    row = lax.broadcasted_iota(jnp.int32, (2 * blk, blk), 0)
    col = lax.broadcasted_iota(jnp.int32, (2 * blk, blk), 1)
    causal = col < jnp.where(row >= blk, row - blk, row)

    for p in range(n_pairs):
        q = q_ref[:, p * LANES:(p + 1) * LANES] * (-(SB_HEAD_DIM ** -0.5))
        zero = jnp.zeros_like(q)
        qs_ref[p] = jnp.concatenate([jnp.where(first_head, q, zero), jnp.where(first_head, zero, q)],
                                    axis=0)

    def phase(key_blocks, with_diag):
        pairs = range(n_pairs)
        steps = range(len(key_blocks))
        starts = [pl.multiple_of(j * blk, blk) for j in key_blocks]
        zns = [[lax.dot_general(qs_ref[p], k_ref[pl.ds(starts[t], blk), p * LANES:(p + 1) * LANES],
                                (((1,), (1,)), ((), ())), preferred_element_type=F32)
                for p in pairs] for t in steps]
        cs = []
        for t in steps:
            diag = with_diag and t == 0
            row_cs = []
            for p in pairs:
                zn = zns[t][p]
                log_1mb = jnp.minimum(zn, 0.0) - jnp.log(1.0 + jnp.exp2(jnp.abs(zn) * (-LOG2E)))
                if diag:
                    log_1mb = jnp.where(causal, log_1mb, 0.0)
                hi = log_1mb.astype(BF16)
                lo = (log_1mb - hi.astype(F32)).astype(BF16)
                row_cs.append(jnp.dot(jnp.concatenate([hi, lo], axis=1), uu_ref[...],
                                      preferred_element_type=F32))
            cs.append(row_cs)
        ws = [[None] * n_pairs for _ in steps]
        run_max = None
        for p in pairs:
            run = None if with_diag else run_ref[p]
            for t in steps:
                c = cs[t][p]
                tail = c[:, :blk] if run is None else c[:, :blk] + run
                run = c[:, blk:] if run is None else run + c[:, blk:]
                w = jnp.exp2((tail - zns[t][p]) * LOG2E)
                if with_diag and t == 0:
                    w = jnp.where(causal, w, 0.0)
                ws[t][p] = w.astype(BF16)
            run_ref[p] = run
            run_max = run if run_max is None else jnp.maximum(run_max, run)
        for p in pairs:
            pv = None
            for t in steps:
                part = jnp.dot(ws[t][p], v_ref[pl.ds(starts[t], blk), p * LANES:(p + 1) * LANES],
                               preferred_element_type=F32)
                pv = part if pv is None else pv + part
            if with_diag:
                acc_ref[p] = pv
            else:
                acc_ref[p] += pv
        return (jnp.max(run_max) > SB_DEAD_LOG).astype(jnp.int32)

    live = phase([i], True)

    def body(carry):
        jj, _ = carry
        return jj + 1, phase([i - 1 - jj], False)

    lax.while_loop(lambda carry: jnp.logical_and(carry[0] < i, carry[1] > 0), body,
                   (jnp.int32(0), live))
    for p in range(n_pairs):
        acc = acc_ref[p]
        o_ref[:, p * LANES:(p + 1) * LANES] = jnp.where(first_head, acc[:blk], acc[blk:]).astype(o_ref.dtype)


def _sb_attention(proj, uu, bsz, seq, n_pairs, pairs_per_step, q_col, k_col, v_col):
    nq = seq // SB_BLOCK
    gp = pairs_per_step
    width = gp * LANES
    return pl.pallas_call(
        functools.partial(_sb_kernel, n_pairs=gp),
        grid=(bsz, n_pairs // gp, nq),
        in_specs=[
            pl.BlockSpec((SB_BLOCK, width), lambda b, p, i: (b * nq + i, q_col + p)),
            pl.BlockSpec((seq, width), lambda b, p, i: (b, k_col + p)),
            pl.BlockSpec((seq, width), lambda b, p, i: (b, v_col + p)),
            _resident((2 * SB_BLOCK, 2 * SB_BLOCK)),
        ],
        out_specs=pl.BlockSpec((SB_BLOCK, width), lambda b, p, i: (b * nq + i, p)),
        out_shape=jax.ShapeDtypeStruct((bsz * seq, n_pairs * LANES), BF16),
        scratch_shapes=[pltpu.VMEM((gp, 2 * SB_BLOCK, LANES), BF16),
                        pltpu.VMEM((gp, 2 * SB_BLOCK, LANES), F32),
                        pltpu.VMEM((gp, 2 * SB_BLOCK, SB_BLOCK), F32)],
        compiler_params=_cparams("parallel", "parallel", "arbitrary"),
        name="sb_attention",
    )(proj, proj, proj, uu)


def _ssd_kernel(xbc_ref, z_ref, dt_ref, dtt_ref, convw_ref, convb_ref, dtb_ref, dtbt_ref,
                alog_ref, alogt_ref, dskip_ref, gnorm_ref, tri_ref, trit_ref, shift_ref,
                o_ref, state_ref, prev_ref, *, inner, n_heads):
    c = pl.program_id(1)
    cl = SSM_CHUNK
    gw = SSM_GROUPS * SSM_STATE
    hpg = n_heads // SSM_GROUPS

    @pl.when(c == 0)
    def _():
        state_ref[...] = jnp.zeros_like(state_ref)
        prev_ref[...] = jnp.zeros_like(prev_ref)

    cur = xbc_ref[...]
    both = jnp.concatenate([prev_ref[...], cur], axis=0)
    prev_ref[...] = cur

    def conv_silu(c0, width):
        cols = slice(c0, c0 + width)
        delayed = jnp.dot(shift_ref[...], both[:, cols], preferred_element_type=F32)
        acc = cur[:, cols].astype(F32) * convw_ref[SSM_CONV - 1:SSM_CONV, cols] + convb_ref[:, cols]
        for k in range(1, SSM_CONV):
            acc = acc + delayed[(k - 1) * cl:k * cl] * convw_ref[SSM_CONV - 1 - k:SSM_CONV - k, cols]
        return acc * _sigmoid(acc)

    bm = conv_silu(inner, gw)
    cm = conv_silu(inner + gw, gw)

    def softplus(v):
        return jnp.maximum(v, 0.0) + jnp.log(1.0 + jnp.exp2(jnp.abs(v) * (-LOG2E)))

    dt_r = softplus(dt_ref[...] + dtb_ref[...])
    dt_t = softplus(dtt_ref[...] + dtbt_ref[...])
    adt_r = dt_r * (-jnp.exp(alog_ref[...]))
    adt_t = dt_t * (-jnp.exp(alogt_ref[...]))
    acs_r = sum(jnp.dot(tri_ref[...], p, preferred_element_type=F32) for p in _split3(adt_r)) * LOG2E
    acs_t = sum(jnp.dot(p, trit_ref[...], preferred_element_type=F32) for p in _split3(adt_t)) * LOG2E
    src_t = acs_t - jnp.log(dt_t) * LOG2E

    lane = lax.broadcasted_iota(jnp.int32, (cl, LANES), 1)
    first_head = lane < SSM_HEAD_DIM
    tl = lax.broadcasted_iota(jnp.int32, (cl, cl), 0)
    ts = lax.broadcasted_iota(jnp.int32, (cl, cl), 1)
    tril = ts <= tl

    gsz = inner // SSM_GROUPS
    for g in range(SSM_GROUPS):
        gcols = slice(g * gsz, (g + 1) * gsz)
        xs = conv_silu(g * gsz, gsz)
        y_pairs = []
        bg = bm[:, g * SSM_STATE:(g + 1) * SSM_STATE]
        cg = cm[:, g * SSM_STATE:(g + 1) * SSM_STATE]
        cb = lax.dot_general(cg.astype(BF16), bg.astype(BF16), (((1,), (1,)), ((), ())),
                             preferred_element_type=F32)
        bgt = bg.T
        for pi in range(hpg // 2):
            h0 = g * hpg + 2 * pi
            lhs_y, lhs_s, cdec = [], [], []
            for h in (h0, h0 + 1):
                a_col = acs_r[:, h:h + 1]
                s_row = src_t[h:h + 1, :]
                a_end = acs_t[h:h + 1, cl - 1:cl]
                m = cb * jnp.exp2(jnp.where(tril, a_col - s_row, -jnp.inf))
                c_scaled = cg * jnp.exp2(a_col)
                lhs_y.append(jnp.concatenate([m.astype(BF16), c_scaled.astype(BF16)], axis=1))
                w_row = jnp.exp2(a_end - s_row)
                lhs_s.append((bgt * w_row).astype(BF16))
                cdec.append(jnp.exp2(a_end))
            cols = slice(h0 * SSM_HEAD_DIM, (h0 + 2) * SSM_HEAD_DIM)
            x_pair = xs[:, 2 * pi * SSM_HEAD_DIM:(2 * pi + 2) * SSM_HEAD_DIM].astype(BF16)
            s_prev = state_ref[:, cols]
            rhs = jnp.concatenate([x_pair, s_prev.astype(BF16)], axis=0)
            out = jnp.dot(jnp.concatenate(lhs_y, axis=0), rhs, preferred_element_type=F32)
            y_pairs.append(jnp.where(first_head, out[:cl], out[cl:]))
            st = jnp.dot(jnp.concatenate(lhs_s, axis=0), x_pair, preferred_element_type=F32)
            st = jnp.where(first_head, st[:SSM_STATE], st[SSM_STATE:])
            state_ref[:, cols] = s_prev * jnp.where(first_head, cdec[0], cdec[1]) + st

        y = jnp.concatenate(y_pairs, axis=1) + dskip_ref[:, gcols] * xs
        zf = z_ref[:, gcols].astype(F32)
        y = y * (zf * _sigmoid(zf))
        y = y * lax.rsqrt(jnp.mean(y * y, axis=-1, keepdims=True) + RMS_EPS)
        o_ref[:, gcols] = (y * gnorm_ref[:, gcols]).astype(o_ref.dtype)


def _ssd(proj, dt, dtt, prm, bsz, seq, inner, n_heads, xbc_col, z_col):
    nc = seq // SSM_CHUNK
    cw = inner + 2 * SSM_GROUPS * SSM_STATE
    const = lambda shape: pl.BlockSpec(shape, lambda b, c: (0, 0))
    kern = functools.partial(_ssd_kernel, inner=inner, n_heads=n_heads)
    return pl.pallas_call(
        kern,
        grid=(bsz, nc),
        in_specs=[
            pl.BlockSpec((SSM_CHUNK, cw), lambda b, c: (b * nc + c, xbc_col)),
            pl.BlockSpec((SSM_CHUNK, inner), lambda b, c: (b * nc + c, z_col)),
            pl.BlockSpec((SSM_CHUNK, LANES), lambda b, c: (b * nc + c, 0)),
            pl.BlockSpec((n_heads, SSM_CHUNK), lambda b, c: (0, b * nc + c)),
            const((SSM_CONV, cw)),
            const((1, cw)),
            const((1, LANES)),
            const((n_heads, SSM_CHUNK)),
            const((1, LANES)),
            const((n_heads, SSM_CHUNK)),
            const((1, inner)),
            const((1, inner)),
            const((SSM_CHUNK, SSM_CHUNK)),
            const((SSM_CHUNK, SSM_CHUNK)),
            const(((SSM_CONV - 1) * SSM_CHUNK, 2 * SSM_CHUNK)),
        ],
        out_specs=pl.BlockSpec((SSM_CHUNK, inner), lambda b, c: (b * nc + c, 0)),
        out_shape=jax.ShapeDtypeStruct((bsz * seq, inner), BF16),
        scratch_shapes=[pltpu.VMEM((SSM_STATE, inner), F32), pltpu.VMEM((SSM_CHUNK, cw), BF16)],
        compiler_params=_cparams("parallel", "arbitrary"),
        name="ssd",
    )(proj, proj, dt, dtt, prm["conv_w"], prm["conv_b"], prm["dt_bias_row"], prm["dt_bias_col"],
      prm["a_log_row"], prm["a_log_col"], prm["d_skip"], prm["g_norm"], prm["tri"], prm["trit"],
      prm["shift"])


def _merge_kernel(x_ref, oa_ref, os_ref, ga_ref, gs_ref, wa_ref, ws_ref, wm_ref, g_ref, o_ref):
    a = jnp.dot(oa_ref[...], wa_ref[...], preferred_element_type=F32)
    s = jnp.dot(os_ref[...], ws_ref[...], preferred_element_type=F32)
    merged = _sigmoid(ga_ref[...].astype(F32)) * a + _sigmoid(gs_ref[...].astype(F32)) * s
    y = jnp.dot(merged.astype(BF16), wm_ref[...], preferred_element_type=F32)
    o_ref[...] = x_ref[...] + _rms(y, g_ref[...])


def _merge(x2, o_att, o_ssm, proj, wa, ws, wm, g, layer, tm, ga_col, gs_col):
    t, d = x2.shape
    inner = o_ssm.shape[1]
    sbw = o_att.shape[1]
    return pl.pallas_call(
        _merge_kernel,
        grid=(t // tm,),
        in_specs=[
            pl.BlockSpec((tm, d), lambda i: (i, 0)),
            pl.BlockSpec((tm, sbw), lambda i: (i, 0)),
            pl.BlockSpec((tm, inner), lambda i: (i, 0)),
            pl.BlockSpec((tm, d), lambda i: (i, ga_col)),
            pl.BlockSpec((tm, d), lambda i: (i, gs_col)),
            _layer_resident((sbw, d), layer),
            _layer_resident((inner, d), layer),
            _layer_resident((d, d), layer),
            _resident((1, d)),
        ],
        out_specs=pl.BlockSpec((tm, d), lambda i: (i, 0)),
        out_shape=jax.ShapeDtypeStruct((t, d), F32),
        compiler_params=_cparams("parallel"),
        name="merge",
    )(x2, o_att, o_ssm, proj, proj, wa, ws, wm, g)


def _xattn_kernel(x_ref, kv_ref, gpre_ref, wq_ref, wo_ref, gpost_ref, o_ref, *, d):
    x = x_ref[...]
    h = _rms(x, gpre_ref[...]).astype(BF16)
    hd = d // XA_HEADS
    q = (jnp.dot(h, wq_ref[...], preferred_element_type=F32) * (hd ** -0.5)).astype(BF16)
    outs = []
    for n in range(XA_HEADS):
        kh = kv_ref[:, n * hd:(n + 1) * hd]
        vh = kv_ref[:, d + n * hd:d + (n + 1) * hd]
        s = lax.dot_general(q[:, n * hd:(n + 1) * hd], kh, (((1,), (1,)), ((), ())),
                            preferred_element_type=F32)
        p = jnp.exp(s - jnp.max(s, axis=-1, keepdims=True))
        denom = jnp.sum(p, axis=-1, keepdims=True)
        o = jnp.dot(p.astype(BF16), vh, preferred_element_type=F32)
        outs.append((o / denom).astype(BF16))
    o = jnp.concatenate(outs, axis=1)
    y = jnp.dot(o, wo_ref[...], preferred_element_type=F32)
    o_ref[...] = x + _rms(y, gpost_ref[...])


def _xattn(x2, kv, gpre, wq, wo, gpost, layer, tm, seq, mem_len):
    t, d = x2.shape
    per_b = seq // tm
    return pl.pallas_call(
        functools.partial(_xattn_kernel, d=d),
        grid=(t // tm,),
        in_specs=[
            pl.BlockSpec((tm, d), lambda i: (i, 0)),
            pl.BlockSpec((mem_len, 2 * d), lambda i: (i // per_b, 0)),
            _resident((1, d)),
            _layer_resident((d, d), layer),
            _layer_resident((d, d), layer),
            _resident((1, d)),
        ],
        out_specs=pl.BlockSpec((tm, d), lambda i: (i, 0)),
        out_shape=jax.ShapeDtypeStruct((t, d), F32),
        compiler_params=_cparams("parallel"),
        name="xattn",
    )(x2, kv, gpre, wq, wo, gpost)


def _ffn_kernel(x_ref, gpre_ref, wgu_ref, wd_ref, gpost_ref, o_ref, *, hidden, chunk):
    x = x_ref[...]
    h = _rms(x, gpre_ref[...]).astype(BF16)
    y = jnp.zeros(x.shape, F32)
    for c0 in range(0, hidden, chunk):
        gate = jnp.dot(h, wgu_ref[:, c0:c0 + chunk], preferred_element_type=F32)
        up = jnp.dot(h, wgu_ref[:, hidden + c0:hidden + c0 + chunk], preferred_element_type=F32)
        act = (gate * _sigmoid(gate) * up).astype(BF16)
        y = y + jnp.dot(act, wd_ref[c0:c0 + chunk, :], preferred_element_type=F32)
    o_ref[...] = x + _rms(y, gpost_ref[...])


def _ffn(x2, gpre, wgu, wd, gpost, layer, tm):
    t, d = x2.shape
    hidden = wd.shape[1]
    chunk = hidden
    for cand in (512, 384, 256, 128):
        if hidden % cand == 0:
            chunk = cand
            break
    return pl.pallas_call(
        functools.partial(_ffn_kernel, hidden=hidden, chunk=chunk),
        grid=(t // tm,),
        in_specs=[
            pl.BlockSpec((tm, d), lambda i: (i, 0)),
            _resident((1, d)),
            _layer_resident((d, 2 * hidden), layer),
            _layer_resident((hidden, d), layer),
            _resident((1, d)),
        ],
        out_specs=pl.BlockSpec((tm, d), lambda i: (i, 0)),
        out_shape=jax.ShapeDtypeStruct((t, d), F32),
        compiler_params=_cparams("parallel"),
        name="ffn",
    )(x2, gpre, wgu, wd, gpost)


def _segment_layout(widths, tile):
    offs, cur = [], 0
    for w in widths:
        cur = -(-cur // w) * w
        offs.append(cur)
        cur += w
    return offs, -(-cur // tile) * tile


def kernel(x, mem, g_pre_mix, w_in, conv_w, conv_b, dt_bias, a_log, d_skip, g_ssm_norm, w_br_att,
           w_br_ssm, w_mix_out, g_post_mix, g_pre_xa, g_mem, w_xq, w_xkv, w_xo, g_post_xa,
           g_pre_ffn, w_gu, w_down, g_post_ffn):
    bsz, seq, d = x.shape
    depth = w_in.shape[0]
    mem_len = mem.shape[1]
    sbw = w_br_att.shape[1]
    inner = w_br_ssm.shape[1]
    n_heads = inner // SSM_HEAD_DIM
    gw = SSM_GROUPS * SSM_STATE
    cw = inner + 2 * gw
    t = bsz * seq
    assert seq % SB_BLOCK == 0 and seq % SSM_CHUNK == 0 and sbw % LANES == 0
    assert n_heads % (2 * SSM_GROUPS) == 0 and n_heads <= LANES and (d // XA_HEADS) % LANES == 0

    src = {}
    cur = 0
    for name, wd_ in (("q", sbw), ("k", sbw), ("v", sbw), ("z", inner), ("xbc", cw),
                      ("dt", n_heads), ("ga", d), ("gs", d)):
        src[name] = (cur, wd_)
        cur += wd_
    order = ("q", "k", "v", "xbc", "z", "ga", "gs")
    tn = 1024
    offs, n_total = _segment_layout([src[k][1] for k in order], tn)
    col = dict(zip(order, offs))

    tm_proj = _pick_tile(t, 2048)
    tm_row = _pick_tile(seq, 512)

    ti = lax.broadcasted_iota(jnp.int32, (SSM_CHUNK, SSM_CHUNK), 0)
    tj = lax.broadcasted_iota(jnp.int32, (SSM_CHUNK, SSM_CHUNK), 1)
    tri = (tj <= ti).astype(BF16)
    trit = (ti <= tj).astype(BF16)
    suffix = (ti >= tj).astype(BF16)
    half = jnp.concatenate([suffix, jnp.ones((SB_BLOCK, SB_BLOCK), BF16)], axis=1)
    uu = jnp.concatenate([half, half], axis=0)
    si = lax.broadcasted_iota(jnp.int32, (SSM_CHUNK, 2 * SSM_CHUNK), 0)
    sj = lax.broadcasted_iota(jnp.int32, (SSM_CHUNK, 2 * SSM_CHUNK), 1)
    shift = jnp.concatenate([(sj == si + SSM_CHUNK - k).astype(BF16) for k in range(1, SSM_CONV)],
                            axis=0)

    x2 = x.reshape(t, d)
    mem2 = mem.reshape(bsz * mem_len, d)
    row = lambda v: v.reshape(1, -1)

    pieces, cur = [], 0
    for k in order:
        s0, wd_ = src[k]
        if col[k] > cur:
            pieces.append(jnp.zeros((depth, d, col[k] - cur), BF16))
        pieces.append(w_in[:, :, s0:s0 + wd_].astype(BF16))
        cur = col[k] + wd_
    if n_total > cur:
        pieces.append(jnp.zeros((depth, d, n_total - cur), BF16))
    w_perm = jnp.concatenate(pieces, axis=2)
    s0, wd_ = src["dt"]
    w_dt = w_in[:, :, s0:s0 + wd_].astype(BF16)
    w_dt_pad = jnp.concatenate([w_dt, jnp.zeros((depth, d, LANES - n_heads), BF16)], axis=2)
    w_dt_t = jnp.swapaxes(w_dt, 1, 2)
    wb = {name: w.astype(BF16) for name, w in (
        ("br_att", w_br_att), ("br_ssm", w_br_ssm), ("mix_out", w_mix_out), ("xkv", w_xkv),
        ("xq", w_xq), ("xo", w_xo), ("gu", w_gu), ("down", w_down))}

    for l in range(depth):
        proj, dt, dtt = _inproj(x2, row(g_pre_mix[l]), w_perm, w_dt_pad, w_dt_t, l, tm_proj, tn)

        n_pairs = sbw // LANES
        gp = SB_PAIRS_PER_STEP if n_pairs % SB_PAIRS_PER_STEP == 0 else 1
        o_att = _sb_attention(proj, uu, bsz, seq, n_pairs, gp, col["q"] // (gp * LANES),
                              col["k"] // (gp * LANES), col["v"] // (gp * LANES))
        pad_row = lambda v: jnp.zeros((1, LANES), F32).at[0, :n_heads].set(v)
        prm = {
            "conv_w": conv_w[l], "conv_b": row(conv_b[l]),
            "dt_bias_row": pad_row(dt_bias[l]),
            "dt_bias_col": jnp.broadcast_to(dt_bias[l][:, None], (n_heads, SSM_CHUNK)),
            "a_log_row": pad_row(a_log[l]),
            "a_log_col": jnp.broadcast_to(a_log[l][:, None], (n_heads, SSM_CHUNK)),
            "d_skip": row(jnp.repeat(d_skip[l], SSM_HEAD_DIM)),
            "g_norm": row(g_ssm_norm[l]),
            "tri": tri, "trit": trit, "shift": shift,
        }
        o_ssm = _ssd(proj, dt, dtt, prm, bsz, seq, inner, n_heads, col["xbc"] // cw, col["z"] // inner)
        x2 = _merge(x2, o_att, o_ssm, proj, wb["br_att"], wb["br_ssm"], wb["mix_out"],
                    row(g_post_mix[l]), l, tm_row, col["ga"] // d, col["gs"] // d)

        kv = _norm_matmul(mem2, row(g_mem[l]), wb["xkv"], l, _pick_tile(bsz * mem_len, 512))
        x2 = _xattn(x2, kv, row(g_pre_xa[l]), wb["xq"], wb["xo"], row(g_post_xa[l]), l, tm_row,
                    seq, mem_len)

        x2 = _ffn(x2, row(g_pre_ffn[l]), wb["gu"], wb["down"], row(g_post_ffn[l]), l, tm_row)
    return x2.reshape(bsz, seq, d)
```

```python
import functools

import jax
import jax.numpy as jnp
from jax import lax
from jax.experimental import pallas as pl
from jax.experimental.pallas import tpu as pltpu

F32 = jnp.float32
BF16 = jnp.bfloat16

RMS_EPS = 1e-6
LOG2E = 1.4426950408889634
SB_HEAD_DIM = 64
SB_BLOCK = 128
SB_PAIRS_PER_STEP = 8
SB_DEAD_LOG = -105.0
SB_EARLY_ROWS = 64
SSM_HEAD_DIM = 64
SSM_GROUPS = 4
SSM_STATE = 128
SSM_CONV = 4
SSM_CHUNK = 128
XA_HEADS = 4
LANES = 128
VMEM_LIMIT_BYTES = 56 * 1024 * 1024


def _cparams(*sem):
    return pltpu.CompilerParams(dimension_semantics=sem, vmem_limit_bytes=VMEM_LIMIT_BYTES)


def _pick_tile(n, want):
    t = min(n, want)
    while n % t:
        t //= 2
    return t


def _resident(shape):
    return pl.BlockSpec(shape, lambda *_: (0,) * len(shape), pipeline_mode=pl.Buffered(1))


def _layer_resident(shape, layer):
    return pl.BlockSpec((None,) + tuple(shape), lambda *_: (layer,) + (0,) * len(shape),
                        pipeline_mode=pl.Buffered(1))


def _rms(x, g):
    ms = jnp.mean(x * x, axis=-1, keepdims=True)
    return x * lax.rsqrt(ms + RMS_EPS) * g


def _sigmoid(x):
    return 1.0 / (1.0 + jnp.exp2(x * (-LOG2E)))


def _split3(x):
    hi = x.astype(BF16)
    r = x - hi.astype(F32)
    mid = r.astype(BF16)
    lo = (r - mid.astype(F32)).astype(BF16)
    return hi, mid, lo


def _inproj_kernel(x_ref, g_ref, w_ref, wdt_ref, wdtt_ref, o_ref, dt_ref, dtt_ref, h_ref):
    @pl.when(pl.program_id(1) == 0)
    def _():
        h = _rms(x_ref[...], g_ref[...]).astype(BF16)
        h_ref[...] = h
        dt_ref[...] = jnp.dot(h, wdt_ref[...], preferred_element_type=F32)
        dtt_ref[...] = lax.dot_general(wdtt_ref[...], h, (((1,), (1,)), ((), ())),
                                       preferred_element_type=F32)

    o_ref[...] = jnp.dot(h_ref[...], w_ref[...], preferred_element_type=F32).astype(o_ref.dtype)


def _inproj(x2, g, w, wdt, wdtt, layer, tm, tn):
    t, d = x2.shape
    n = w.shape[2]
    nh = wdtt.shape[1]
    return pl.pallas_call(
        _inproj_kernel,
        grid=(t // tm, n // tn),
        in_specs=[
            pl.BlockSpec((tm, d), lambda i, j: (i, 0)),
            pl.BlockSpec((1, d), lambda i, j: (0, 0)),
            pl.BlockSpec((None, d, tn), lambda i, j: (layer, 0, j)),
            pl.BlockSpec((None, d, LANES), lambda i, j: (layer, 0, 0)),
            pl.BlockSpec((None, nh, d), lambda i, j: (layer, 0, 0)),
        ],
        out_specs=[
            pl.BlockSpec((tm, tn), lambda i, j: (i, j)),
            pl.BlockSpec((tm, LANES), lambda i, j: (i, 0)),
            pl.BlockSpec((nh, tm), lambda i, j: (0, i)),
        ],
        out_shape=[
            jax.ShapeDtypeStruct((t, n), BF16),
            jax.ShapeDtypeStruct((t, LANES), F32),
            jax.ShapeDtypeStruct((nh, t), F32),
        ],
        scratch_shapes=[pltpu.VMEM((tm, d), BF16)],
        compiler_params=_cparams("parallel", "arbitrary"),
        name="inproj",
    )(x2, g, w, wdt, wdtt)


def _norm_matmul_kernel(x_ref, g_ref, w_ref, o_ref):
    h = _rms(x_ref[...], g_ref[...]).astype(BF16)
    o_ref[...] = jnp.dot(h, w_ref[...], preferred_element_type=F32).astype(o_ref.dtype)


def _norm_matmul(x2, g, w, layer, tm):
    t, d = x2.shape
    n = w.shape[2]
    return pl.pallas_call(
        _norm_matmul_kernel,
        grid=(t // tm,),
        in_specs=[
            pl.BlockSpec((tm, d), lambda i: (i, 0)),
            pl.BlockSpec((1, d), lambda i: (0, 0)),
            _layer_resident((d, n), layer),
        ],
        out_specs=pl.BlockSpec((tm, n), lambda i: (i, 0)),
        out_shape=jax.ShapeDtypeStruct((t, n), BF16),
        compiler_params=_cparams("parallel"),
        name="mem_kv",
    )(x2, g, w)


def _sb_kernel(q_ref, k_ref, v_ref, uu_ref, o_ref, qs_ref, acc_ref, run_ref, *, n_pairs):
    i = pl.program_id(2)
    blk = SB_BLOCK
    lane = lax.broadcasted_iota(jnp.int32, (blk, LANES), 1)
    first_head = lane < SB_HEAD_DIM
    row = lax.broadcasted_iota(jnp.int32, (2 * blk, blk), 0)
    col = lax.broadcasted_iota(jnp.int32, (2 * blk, blk), 1)
    causal = col < jnp.where(row >= blk, row - blk, row)

    for p in range(n_pairs):
        q = q_ref[:, p * LANES:(p + 1) * LANES] * (-(SB_HEAD_DIM ** -0.5))
        zero = jnp.zeros_like(q)
        qs_ref[p] = jnp.concatenate([jnp.where(first_head, q, zero), jnp.where(first_head, zero, q)],
                                    axis=0)

    hr = SB_EARLY_ROWS

    def take(ref, p, early_only):
        if not early_only:
            return ref[p]
        return jnp.concatenate([ref[p, 0:hr, :], ref[p, blk:blk + hr, :]], axis=0)

    def put(ref, p, val, early_only):
        if not early_only:
            ref[p] = val
        else:
            ref[p, 0:hr, :] = val[:hr]
            ref[p, blk:blk + hr, :] = val[hr:]

    def any_live(run_max):
        return (jnp.max(run_max) > SB_DEAD_LOG).astype(jnp.int32)

    def phase(j, diag, early_only):
        pairs = range(n_pairs)
        start = pl.multiple_of(j * blk, blk)
        zns = [lax.dot_general(take(qs_ref, p, early_only),
                               k_ref[pl.ds(start, blk), p * LANES:(p + 1) * LANES],
                               (((1,), (1,)), ((), ())), preferred_element_type=F32) for p in pairs]
        cs = []
        for p in pairs:
            zn = zns[p]
            log_1mb = jnp.minimum(zn, 0.0) - jnp.log(1.0 + jnp.exp2(jnp.abs(zn) * (-LOG2E)))
            if diag:
                log_1mb = jnp.where(causal, log_1mb, 0.0)
            hi = log_1mb.astype(BF16)
            lo = (log_1mb - hi.astype(F32)).astype(BF16)
            cs.append(jnp.dot(jnp.concatenate([hi, lo], axis=1), uu_ref[...],
                              preferred_element_type=F32))
        ws = []
        run_max = None
        for p in pairs:
            c = cs[p]
            if diag:
                tail = c[:, :blk]
                run = c[:, blk:]
            else:
                run = take(run_ref, p, early_only)
                tail = c[:, :blk] + run
                run = run + c[:, blk:]
            put(run_ref, p, run, early_only)
            run_max = run if run_max is None else jnp.maximum(run_max, run)
            w = jnp.exp2((tail - zns[p]) * LOG2E)
            if diag:
                w = jnp.where(causal, w, 0.0)
            ws.append(w.astype(BF16))
        for p in pairs:
            pv = jnp.dot(ws[p], v_ref[pl.ds(start, blk), p * LANES:(p + 1) * LANES],
                         preferred_element_type=F32)
            if diag:
                acc_ref[p] = pv
            else:
                put(acc_ref, p, take(acc_ref, p, early_only) + pv, early_only)
        if early_only:
            return any_live(run_max), jnp.int32(0)
        early = jnp.maximum(run_max[0:hr], run_max[blk:blk + hr])
        later = jnp.maximum(run_max[hr:blk], run_max[blk + hr:])
        return any_live(early), any_live(later)

    early_live, later_live = phase(i, True, False)

    def full_body(carry):
        jj, _, _ = carry
        return (jj + 1,) + phase(i - 1 - jj, False, False)

    jj, early_live, _ = lax.while_loop(lambda c: jnp.logical_and(c[0] < i, c[2] > 0), full_body,
                                       (jnp.int32(0), early_live, later_live))

    def early_body(carry):
        jj, _ = carry
        return jj + 1, phase(i - 1 - jj, False, True)[0]

    lax.while_loop(lambda c: jnp.logical_and(c[0] < i, c[1] > 0), early_body, (jj, early_live))
    for p in range(n_pairs):
        acc = acc_ref[p]
        o_ref[:, p * LANES:(p + 1) * LANES] = jnp.where(first_head, acc[:blk], acc[blk:]).astype(o_ref.dtype)


def _sb_attention(proj, uu, bsz, seq, n_pairs, pairs_per_step, q_col, k_col, v_col):
    nq = seq // SB_BLOCK
    gp = pairs_per_step
    width = gp * LANES
    return pl.pallas_call(
        functools.partial(_sb_kernel, n_pairs=gp),
        grid=(bsz, n_pairs // gp, nq),
        in_specs=[
            pl.BlockSpec((SB_BLOCK, width), lambda b, p, i: (b * nq + i, q_col + p)),
            pl.BlockSpec((seq, width), lambda b, p, i: (b, k_col + p)),
            pl.BlockSpec((seq, width), lambda b, p, i: (b, v_col + p)),
            _resident((2 * SB_BLOCK, 2 * SB_BLOCK)),
        ],
        out_specs=pl.BlockSpec((SB_BLOCK, width), lambda b, p, i: (b * nq + i, p)),
        out_shape=jax.ShapeDtypeStruct((bsz * seq, n_pairs * LANES), BF16),
        scratch_shapes=[pltpu.VMEM((gp, 2 * SB_BLOCK, LANES), BF16),
                        pltpu.VMEM((gp, 2 * SB_BLOCK, LANES), F32),
                        pltpu.VMEM((gp, 2 * SB_BLOCK, SB_BLOCK), F32)],
        compiler_params=_cparams("parallel", "parallel", "arbitrary"),
        name="sb_attention",
    )(proj, proj, proj, uu)


def _ssd_kernel(xbc_ref, z_ref, dt_ref, dtt_ref, convw_ref, convb_ref, dtb_ref, dtbt_ref,
                alog_ref, alogt_ref, dskip_ref, gnorm_ref, tri_ref, trit_ref, shift_ref,
                o_ref, state_ref, prev_ref, *, inner, n_heads):
    c = pl.program_id(1)
    cl = SSM_CHUNK
    gw = SSM_GROUPS * SSM_STATE
    hpg = n_heads // SSM_GROUPS

    @pl.when(c == 0)
    def _():
        state_ref[...] = jnp.zeros_like(state_ref)
        prev_ref[...] = jnp.zeros_like(prev_ref)

    cur = xbc_ref[...]
    both = jnp.concatenate([prev_ref[...], cur], axis=0)
    prev_ref[...] = cur

    def conv_silu(c0, width):
        cols = slice(c0, c0 + width)
        delayed = jnp.dot(shift_ref[...], both[:, cols], preferred_element_type=F32)
        acc = cur[:, cols].astype(F32) * convw_ref[SSM_CONV - 1:SSM_CONV, cols] + convb_ref[:, cols]
        for k in range(1, SSM_CONV):
            acc = acc + delayed[(k - 1) * cl:k * cl] * convw_ref[SSM_CONV - 1 - k:SSM_CONV - k, cols]
        return acc * _sigmoid(acc)

    bm = conv_silu(inner, gw)
    cm = conv_silu(inner + gw, gw)

    def softplus(v):
        return jnp.maximum(v, 0.0) + jnp.log(1.0 + jnp.exp2(jnp.abs(v) * (-LOG2E)))

    dt_r = softplus(dt_ref[...] + dtb_ref[...])
    dt_t = softplus(dtt_ref[...] + dtbt_ref[...])
    adt_r = dt_r * (-jnp.exp(alog_ref[...]))
    adt_t = dt_t * (-jnp.exp(alogt_ref[...]))
    acs_r = sum(jnp.dot(tri_ref[...], p, preferred_element_type=F32) for p in _split3(adt_r)) * LOG2E
    acs_t = sum(jnp.dot(p, trit_ref[...], preferred_element_type=F32) for p in _split3(adt_t)) * LOG2E
    src_t = acs_t - jnp.log(dt_t) * LOG2E

    lane = lax.broadcasted_iota(jnp.int32, (cl, LANES), 1)
    first_head = lane < SSM_HEAD_DIM
    tl = lax.broadcasted_iota(jnp.int32, (cl, cl), 0)
    ts = lax.broadcasted_iota(jnp.int32, (cl, cl), 1)
    tril = ts <= tl

    gsz = inner // SSM_GROUPS
    for g in range(SSM_GROUPS):
        gcols = slice(g * gsz, (g + 1) * gsz)
        xs = conv_silu(g * gsz, gsz)
        y_pairs = []
        bg = bm[:, g * SSM_STATE:(g + 1) * SSM_STATE]
        cg = cm[:, g * SSM_STATE:(g + 1) * SSM_STATE]
        cb = lax.dot_general(cg.astype(BF16), bg.astype(BF16), (((1,), (1,)), ((), ())),
                             preferred_element_type=F32)
        bgt = bg.T
        for pi in range(hpg // 2):
            h0 = g * hpg + 2 * pi
            lhs_y, lhs_s, cdec = [], [], []
            for h in (h0, h0 + 1):
                a_col = acs_r[:, h:h + 1]
                s_row = src_t[h:h + 1, :]
                a_end = acs_t[h:h + 1, cl - 1:cl]
                m = cb * jnp.exp2(jnp.where(tril, a_col - s_row, -jnp.inf))
                c_scaled = cg * jnp.exp2(a_col)
                lhs_y.append(jnp.concatenate([m.astype(BF16), c_scaled.astype(BF16)], axis=1))
                w_row = jnp.exp2(a_end - s_row)
                lhs_s.append((bgt * w_row).astype(BF16))
                cdec.append(jnp.exp2(a_end))
            cols = slice(h0 * SSM_HEAD_DIM, (h0 + 2) * SSM_HEAD_DIM)
            x_pair = xs[:, 2 * pi * SSM_HEAD_DIM:(2 * pi + 2) * SSM_HEAD_DIM].astype(BF16)
            s_prev = state_ref[:, cols]
            rhs = jnp.concatenate([x_pair, s_prev.astype(BF16)], axis=0)
            out = jnp.dot(jnp.concatenate(lhs_y, axis=0), rhs, preferred_element_type=F32)
            y_pairs.append(jnp.where(first_head, out[:cl], out[cl:]))
            st = jnp.dot(jnp.concatenate(lhs_s, axis=0), x_pair, preferred_element_type=F32)
            st = jnp.where(first_head, st[:SSM_STATE], st[SSM_STATE:])
            state_ref[:, cols] = s_prev * jnp.where(first_head, cdec[0], cdec[1]) + st

        y = jnp.concatenate(y_pairs, axis=1) + dskip_ref[:, gcols] * xs
        zf = z_ref[:, gcols].astype(F32)
        y = y * (zf * _sigmoid(zf))
        y = y * lax.rsqrt(jnp.mean(y * y, axis=-1, keepdims=True) + RMS_EPS)
        o_ref[:, gcols] = (y * gnorm_ref[:, gcols]).astype(o_ref.dtype)


def _ssd(proj, dt, dtt, prm, bsz, seq, inner, n_heads, xbc_col, z_col):
    nc = seq // SSM_CHUNK
    cw = inner + 2 * SSM_GROUPS * SSM_STATE
    const = lambda shape: pl.BlockSpec(shape, lambda b, c: (0, 0))
    kern = functools.partial(_ssd_kernel, inner=inner, n_heads=n_heads)
    return pl.pallas_call(
        kern,
        grid=(bsz, nc),
        in_specs=[
            pl.BlockSpec((SSM_CHUNK, cw), lambda b, c: (b * nc + c, xbc_col)),
            pl.BlockSpec((SSM_CHUNK, inner), lambda b, c: (b * nc + c, z_col)),
            pl.BlockSpec((SSM_CHUNK, LANES), lambda b, c: (b * nc + c, 0)),
            pl.BlockSpec((n_heads, SSM_CHUNK), lambda b, c: (0, b * nc + c)),
            const((SSM_CONV, cw)),
            const((1, cw)),
            const((1, LANES)),
            const((n_heads, SSM_CHUNK)),
            const((1, LANES)),
            const((n_heads, SSM_CHUNK)),
            const((1, inner)),
            const((1, inner)),
            const((SSM_CHUNK, SSM_CHUNK)),
            const((SSM_CHUNK, SSM_CHUNK)),
            const(((SSM_CONV - 1) * SSM_CHUNK, 2 * SSM_CHUNK)),
        ],
        out_specs=pl.BlockSpec((SSM_CHUNK, inner), lambda b, c: (b * nc + c, 0)),
        out_shape=jax.ShapeDtypeStruct((bsz * seq, inner), BF16),
        scratch_shapes=[pltpu.VMEM((SSM_STATE, inner), F32), pltpu.VMEM((SSM_CHUNK, cw), BF16)],
        compiler_params=_cparams("parallel", "arbitrary"),
        name="ssd",
    )(proj, proj, dt, dtt, prm["conv_w"], prm["conv_b"], prm["dt_bias_row"], prm["dt_bias_col"],
      prm["a_log_row"], prm["a_log_col"], prm["d_skip"], prm["g_norm"], prm["tri"], prm["trit"],
      prm["shift"])


def _merge_kernel(x_ref, oa_ref, os_ref, ga_ref, gs_ref, wa_ref, ws_ref, wm_ref, g_ref, o_ref):
    a = jnp.dot(oa_ref[...], wa_ref[...], preferred_element_type=F32)
    s = jnp.dot(os_ref[...], ws_ref[...], preferred_element_type=F32)
    merged = _sigmoid(ga_ref[...].astype(F32)) * a + _sigmoid(gs_ref[...].astype(F32)) * s
    y = jnp.dot(merged.astype(BF16), wm_ref[...], preferred_element_type=F32)
    o_ref[...] = x_ref[...] + _rms(y, g_ref[...])


def _merge(x2, o_att, o_ssm, proj, wa, ws, wm, g, layer, tm, ga_col, gs_col):
    t, d = x2.shape
    inner = o_ssm.shape[1]
    sbw = o_att.shape[1]
    return pl.pallas_call(
        _merge_kernel,
        grid=(t // tm,),
        in_specs=[
            pl.BlockSpec((tm, d), lambda i: (i, 0)),
            pl.BlockSpec((tm, sbw), lambda i: (i, 0)),
            pl.BlockSpec((tm, inner), lambda i: (i, 0)),
            pl.BlockSpec((tm, d), lambda i: (i, ga_col)),
            pl.BlockSpec((tm, d), lambda i: (i, gs_col)),
            _layer_resident((sbw, d), layer),
            _layer_resident((inner, d), layer),
            _layer_resident((d, d), layer),
            _resident((1, d)),
        ],
        out_specs=pl.BlockSpec((tm, d), lambda i: (i, 0)),
        out_shape=jax.ShapeDtypeStruct((t, d), F32),
        compiler_params=_cparams("parallel"),
        name="merge",
    )(x2, o_att, o_ssm, proj, proj, wa, ws, wm, g)


def _xattn_kernel(x_ref, kv_ref, gpre_ref, wq_ref, wo_ref, gpost_ref, o_ref, *, d):
    x = x_ref[...]
    h = _rms(x, gpre_ref[...]).astype(BF16)
    hd = d // XA_HEADS
    q = (jnp.dot(h, wq_ref[...], preferred_element_type=F32) * (hd ** -0.5)).astype(BF16)
    outs = []
    for n in range(XA_HEADS):
        kh = kv_ref[:, n * hd:(n + 1) * hd]
        vh = kv_ref[:, d + n * hd:d + (n + 1) * hd]
        s = lax.dot_general(q[:, n * hd:(n + 1) * hd], kh, (((1,), (1,)), ((), ())),
                            preferred_element_type=F32)
        p = jnp.exp(s - jnp.max(s, axis=-1, keepdims=True))
        denom = jnp.sum(p, axis=-1, keepdims=True)
        o = jnp.dot(p.astype(BF16), vh, preferred_element_type=F32)
        outs.append((o / denom).astype(BF16))
    o = jnp.concatenate(outs, axis=1)
    y = jnp.dot(o, wo_ref[...], preferred_element_type=F32)
    o_ref[...] = x + _rms(y, gpost_ref[...])


def _xattn(x2, kv, gpre, wq, wo, gpost, layer, tm, seq, mem_len):
    t, d = x2.shape
    per_b = seq // tm
    return pl.pallas_call(
        functools.partial(_xattn_kernel, d=d),
        grid=(t // tm,),
        in_specs=[
            pl.BlockSpec((tm, d), lambda i: (i, 0)),
            pl.BlockSpec((mem_len, 2 * d), lambda i: (i // per_b, 0)),
            _resident((1, d)),
            _layer_resident((d, d), layer),
            _layer_resident((d, d), layer),
            _resident((1, d)),
        ],
        out_specs=pl.BlockSpec((tm, d), lambda i: (i, 0)),
        out_shape=jax.ShapeDtypeStruct((t, d), F32),
        compiler_params=_cparams("parallel"),
        name="xattn",
    )(x2, kv, gpre, wq, wo, gpost)


def _ffn_kernel(x_ref, gpre_ref, wgu_ref, wd_ref, gpost_ref, o_ref, *, hidden, chunk):
    x = x_ref[...]
    h = _rms(x, gpre_ref[...]).astype(BF16)
    y = jnp.zeros(x.shape, F32)
    for c0 in range(0, hidden, chunk):
        gate = jnp.dot(h, wgu_ref[:, c0:c0 + chunk], preferred_element_type=F32)
        up = jnp.dot(h, wgu_ref[:, hidden + c0:hidden + c0 + chunk], preferred_element_type=F32)
        act = (gate * _sigmoid(gate) * up).astype(BF16)
        y = y + jnp.dot(act, wd_ref[c0:c0 + chunk, :], preferred_element_type=F32)
    o_ref[...] = x + _rms(y, gpost_ref[...])


def _ffn(x2, gpre, wgu, wd, gpost, layer, tm):
    t, d = x2.shape
    hidden = wd.shape[1]
    chunk = hidden
    for cand in (512, 384, 256, 128):
        if hidden % cand == 0:
            chunk = cand
            break
    return pl.pallas_call(
        functools.partial(_ffn_kernel, hidden=hidden, chunk=chunk),
        grid=(t // tm,),
        in_specs=[
            pl.BlockSpec((tm, d), lambda i: (i, 0)),
            _resident((1, d)),
            _layer_resident((d, 2 * hidden), layer),
            _layer_resident((hidden, d), layer),
            _resident((1, d)),
        ],
        out_specs=pl.BlockSpec((tm, d), lambda i: (i, 0)),
        out_shape=jax.ShapeDtypeStruct((t, d), F32),
        compiler_params=_cparams("parallel"),
        name="ffn",
    )(x2, gpre, wgu, wd, gpost)


def _segment_layout(widths, tile):
    offs, cur = [], 0
    for w in widths:
        cur = -(-cur // w) * w
        offs.append(cur)
        cur += w
    return offs, -(-cur // tile) * tile


def kernel(x, mem, g_pre_mix, w_in, conv_w, conv_b, dt_bias, a_log, d_skip, g_ssm_norm, w_br_att,
           w_br_ssm, w_mix_out, g_post_mix, g_pre_xa, g_mem, w_xq, w_xkv, w_xo, g_post_xa,
           g_pre_ffn, w_gu, w_down, g_post_ffn):
    bsz, seq, d = x.shape
    depth = w_in.shape[0]
    mem_len = mem.shape[1]
    sbw = w_br_att.shape[1]
    inner = w_br_ssm.shape[1]
    n_heads = inner // SSM_HEAD_DIM
    gw = SSM_GROUPS * SSM_STATE
    cw = inner + 2 * gw
    t = bsz * seq
    assert seq % SB_BLOCK == 0 and seq % SSM_CHUNK == 0 and sbw % LANES == 0
    assert n_heads % (2 * SSM_GROUPS) == 0 and n_heads <= LANES and (d // XA_HEADS) % LANES == 0

    src = {}
    cur = 0
    for name, wd_ in (("q", sbw), ("k", sbw), ("v", sbw), ("z", inner), ("xbc", cw),
                      ("dt", n_heads), ("ga", d), ("gs", d)):
        src[name] = (cur, wd_)
        cur += wd_
    order = ("q", "k", "v", "xbc", "z", "ga", "gs")
    tn = 1024
    offs, n_total = _segment_layout([src[k][1] for k in order], tn)
    col = dict(zip(order, offs))

    tm_proj = _pick_tile(t, 2048)
    tm_row = _pick_tile(seq, 512)

    ti = lax.broadcasted_iota(jnp.int32, (SSM_CHUNK, SSM_CHUNK), 0)
    tj = lax.broadcasted_iota(jnp.int32, (SSM_CHUNK, SSM_CHUNK), 1)
    tri = (tj <= ti).astype(BF16)
    trit = (ti <= tj).astype(BF16)
    suffix = (ti >= tj).astype(BF16)
    half = jnp.concatenate([suffix, jnp.ones((SB_BLOCK, SB_BLOCK), BF16)], axis=1)
    uu = jnp.concatenate([half, half], axis=0)
    si = lax.broadcasted_iota(jnp.int32, (SSM_CHUNK, 2 * SSM_CHUNK), 0)
    sj = lax.broadcasted_iota(jnp.int32, (SSM_CHUNK, 2 * SSM_CHUNK), 1)
    shift = jnp.concatenate([(sj == si + SSM_CHUNK - k).astype(BF16) for k in range(1, SSM_CONV)],
                            axis=0)

    x2 = x.reshape(t, d)
    mem2 = mem.reshape(bsz * mem_len, d)
    row = lambda v: v.reshape(1, -1)

    pieces, cur = [], 0
    for k in order:
        s0, wd_ = src[k]
        if col[k] > cur:
            pieces.append(jnp.zeros((depth, d, col[k] - cur), BF16))
        pieces.append(w_in[:, :, s0:s0 + wd_].astype(BF16))
        cur = col[k] + wd_
    if n_total > cur:
        pieces.append(jnp.zeros((depth, d, n_total - cur), BF16))
    w_perm = jnp.concatenate(pieces, axis=2)
    s0, wd_ = src["dt"]
    w_dt = w_in[:, :, s0:s0 + wd_].astype(BF16)
    w_dt_pad = jnp.concatenate([w_dt, jnp.zeros((depth, d, LANES - n_heads), BF16)], axis=2)
    w_dt_t = jnp.swapaxes(w_dt, 1, 2)
    wb = {name: w.astype(BF16) for name, w in (
        ("br_att", w_br_att), ("br_ssm", w_br_ssm), ("mix_out", w_mix_out), ("xkv", w_xkv),
        ("xq", w_xq), ("xo", w_xo), ("gu", w_gu), ("down", w_down))}

    for l in range(depth):
        proj, dt, dtt = _inproj(x2, row(g_pre_mix[l]), w_perm, w_dt_pad, w_dt_t, l, tm_proj, tn)

        n_pairs = sbw // LANES
        gp = SB_PAIRS_PER_STEP if n_pairs % SB_PAIRS_PER_STEP == 0 else 1
        o_att = _sb_attention(proj, uu, bsz, seq, n_pairs, gp, col["q"] // (gp * LANES),
                              col["k"] // (gp * LANES), col["v"] // (gp * LANES))
        pad_row = lambda v: jnp.zeros((1, LANES), F32).at[0, :n_heads].set(v)
        prm = {
            "conv_w": conv_w[l], "conv_b": row(conv_b[l]),
            "dt_bias_row": pad_row(dt_bias[l]),
            "dt_bias_col": jnp.broadcast_to(dt_bias[l][:, None], (n_heads, SSM_CHUNK)),
            "a_log_row": pad_row(a_log[l]),
            "a_log_col": jnp.broadcast_to(a_log[l][:, None], (n_heads, SSM_CHUNK)),
            "d_skip": row(jnp.repeat(d_skip[l], SSM_HEAD_DIM)),
            "g_norm": row(g_ssm_norm[l]),
            "tri": tri, "trit": trit, "shift": shift,
        }
        o_ssm = _ssd(proj, dt, dtt, prm, bsz, seq, inner, n_heads, col["xbc"] // cw, col["z"] // inner)
        x2 = _merge(x2, o_att, o_ssm, proj, wb["br_att"], wb["br_ssm"], wb["mix_out"],
                    row(g_post_mix[l]), l, tm_row, col["ga"] // d, col["gs"] // d)

        kv = _norm_matmul(mem2, row(g_mem[l]), wb["xkv"], l, _pick_tile(bsz * mem_len, 512))
        x2 = _xattn(x2, kv, row(g_pre_xa[l]), wb["xq"], wb["xo"], row(g_post_xa[l]), l, tm_row,
                    seq, mem_len)

        x2 = _ffn(x2, row(g_pre_ffn[l]), wb["gu"], wb["down"], row(g_post_ffn[l]), l, tm_row)
    return x2.reshape(bsz, seq, d)
```

```python
import functools

import jax
import jax.numpy as jnp
from jax import lax
from jax.experimental import pallas as pl
from jax.experimental.pallas import tpu as pltpu

F32 = jnp.float32
BF16 = jnp.bfloat16

RMS_EPS = 1e-6
LOG2E = 1.4426950408889634
SB_HEAD_DIM = 64
SB_BLOCK = 128
SB_PAIRS_PER_STEP = 8
SB_DEAD_LOG = -105.0
SB_EARLY_ROWS = 64
SSM_HEAD_DIM = 64
SSM_GROUPS = 4
SSM_STATE = 128
SSM_CONV = 4
SSM_CHUNK = 128
XA_HEADS = 4
LANES = 128
VMEM_LIMIT_BYTES = 56 * 1024 * 1024


def _cparams(*sem):
    return pltpu.CompilerParams(dimension_semantics=sem, vmem_limit_bytes=VMEM_LIMIT_BYTES)


def _pick_tile(n, want):
    t = min(n, want)
    while n % t:
        t //= 2
    return t


def _resident(shape):
    return pl.BlockSpec(shape, lambda *_: (0,) * len(shape), pipeline_mode=pl.Buffered(1))


def _layer_resident(shape, layer):
    return pl.BlockSpec((None,) + tuple(shape), lambda *_: (layer,) + (0,) * len(shape),
                        pipeline_mode=pl.Buffered(1))


def _rms(x, g):
    ms = jnp.mean(x * x, axis=-1, keepdims=True)
    return x * lax.rsqrt(ms + RMS_EPS) * g


def _sigmoid(x):
    return 1.0 / (1.0 + jnp.exp2(x * (-LOG2E)))


def _split3(x):
    hi = x.astype(BF16)
    r = x - hi.astype(F32)
    mid = r.astype(BF16)
    lo = (r - mid.astype(F32)).astype(BF16)
    return hi, mid, lo


def _inproj_kernel(x_ref, g_ref, w_ref, wdt_ref, o_ref, dt_ref, dtt_ref, h_ref):
    @pl.when(pl.program_id(1) == 0)
    def _():
        h = _rms(x_ref[...], g_ref[...]).astype(BF16)
        h_ref[...] = h
        dt = jnp.dot(h, wdt_ref[...], preferred_element_type=F32)
        dt_ref[...] = dt
        dtt_ref[...] = dt.T[:dtt_ref.shape[0], :]

    o_ref[...] = jnp.dot(h_ref[...], w_ref[...], preferred_element_type=F32).astype(o_ref.dtype)


def _inproj(x2, g, w, wdt, nh, layer, tm, tn):
    t, d = x2.shape
    n = w.shape[2]
    return pl.pallas_call(
        _inproj_kernel,
        grid=(t // tm, n // tn),
        in_specs=[
            pl.BlockSpec((tm, d), lambda i, j: (i, 0)),
            pl.BlockSpec((1, d), lambda i, j: (0, 0)),
            pl.BlockSpec((None, d, tn), lambda i, j: (layer, 0, j)),
            pl.BlockSpec((None, d, LANES), lambda i, j: (layer, 0, 0)),
        ],
        out_specs=[
            pl.BlockSpec((tm, tn), lambda i, j: (i, j)),
            pl.BlockSpec((tm, LANES), lambda i, j: (i, 0)),
            pl.BlockSpec((nh, tm), lambda i, j: (0, i)),
        ],
        out_shape=[
            jax.ShapeDtypeStruct((t, n), BF16),
            jax.ShapeDtypeStruct((t, LANES), F32),
            jax.ShapeDtypeStruct((nh, t), F32),
        ],
        scratch_shapes=[pltpu.VMEM((tm, d), BF16)],
        compiler_params=_cparams("parallel", "arbitrary"),
        name="inproj",
    )(x2, g, w, wdt)


def _norm_matmul_kernel(x_ref, g_ref, w_ref, o_ref):
    h = _rms(x_ref[...], g_ref[...]).astype(BF16)
    o_ref[...] = jnp.dot(h, w_ref[...], preferred_element_type=F32).astype(o_ref.dtype)


def _norm_matmul(x2, g, w, layer, tm):
    t, d = x2.shape
    n = w.shape[2]
    return pl.pallas_call(
        _norm_matmul_kernel,
        grid=(t // tm,),
        in_specs=[
            pl.BlockSpec((tm, d), lambda i: (i, 0)),
            pl.BlockSpec((1, d), lambda i: (0, 0)),
            _layer_resident((d, n), layer),
        ],
        out_specs=pl.BlockSpec((tm, n), lambda i: (i, 0)),
        out_shape=jax.ShapeDtypeStruct((t, n), BF16),
        compiler_params=_cparams("parallel"),
        name="mem_kv",
    )(x2, g, w)


def _sb_kernel(q_ref, k_ref, v_ref, uu_ref, o_ref, qs_ref, acc_ref, run_ref, *, n_pairs):
    i = pl.program_id(2)
    blk = SB_BLOCK
    lane = lax.broadcasted_iota(jnp.int32, (blk, LANES), 1)
    first_head = lane < SB_HEAD_DIM
    row = lax.broadcasted_iota(jnp.int32, (2 * blk, blk), 0)
    col = lax.broadcasted_iota(jnp.int32, (2 * blk, blk), 1)
    causal = col < jnp.where(row >= blk, row - blk, row)

    for p in range(n_pairs):
        q = q_ref[:, p * LANES:(p + 1) * LANES] * (-(SB_HEAD_DIM ** -0.5))
        zero = jnp.zeros_like(q)
        qs_ref[p] = jnp.concatenate([jnp.where(first_head, q, zero), jnp.where(first_head, zero, q)],
                                    axis=0)

    hr = SB_EARLY_ROWS

    def take(ref, p, early_only):
        if not early_only:
            return ref[p]
        return jnp.concatenate([ref[p, 0:hr, :], ref[p, blk:blk + hr, :]], axis=0)

    def put(ref, p, val, early_only):
        if not early_only:
            ref[p] = val
        else:
            ref[p, 0:hr, :] = val[:hr]
            ref[p, blk:blk + hr, :] = val[hr:]

    def any_live(run_max):
        return (jnp.max(run_max) > SB_DEAD_LOG).astype(jnp.int32)

    def phase(j, diag, early_only):
        pairs = range(n_pairs)
        start = pl.multiple_of(j * blk, blk)
        zns = [lax.dot_general(take(qs_ref, p, early_only),
                               k_ref[pl.ds(start, blk), p * LANES:(p + 1) * LANES],
                               (((1,), (1,)), ((), ())), preferred_element_type=F32) for p in pairs]
        cs = []
        for p in pairs:
            zn = zns[p]
            log_1mb = jnp.minimum(zn, 0.0) - jnp.log(1.0 + jnp.exp2(jnp.abs(zn) * (-LOG2E)))
            if diag:
                log_1mb = jnp.where(causal, log_1mb, 0.0)
            hi = log_1mb.astype(BF16)
            lo = (log_1mb - hi.astype(F32)).astype(BF16)
            cs.append(jnp.dot(jnp.concatenate([hi, lo], axis=1), uu_ref[...],
                              preferred_element_type=F32))
        ws = []
        run_max = None
        for p in pairs:
            c = cs[p]
            if diag:
                tail = c[:, :blk]
                run = c[:, blk:]
            else:
                run = take(run_ref, p, early_only)
                tail = c[:, :blk] + run
                run = run + c[:, blk:]
            put(run_ref, p, run, early_only)
            run_max = run if run_max is None else jnp.maximum(run_max, run)
            w = jnp.exp2((tail - zns[p]) * LOG2E)
            if diag:
                w = jnp.where(causal, w, 0.0)
            ws.append(w.astype(BF16))
        for p in pairs:
            pv = jnp.dot(ws[p], v_ref[pl.ds(start, blk), p * LANES:(p + 1) * LANES],
                         preferred_element_type=F32)
            if diag:
                acc_ref[p] = pv
            else:
                put(acc_ref, p, take(acc_ref, p, early_only) + pv, early_only)
        if early_only:
            return any_live(run_max), jnp.int32(0)
        early = jnp.maximum(run_max[0:hr], run_max[blk:blk + hr])
        later = jnp.maximum(run_max[hr:blk], run_max[blk + hr:])
        return any_live(early), any_live(later)

    early_live, later_live = phase(i, True, False)

    def full_body(carry):
        jj, _, _ = carry
        return (jj + 1,) + phase(i - 1 - jj, False, False)

    jj, early_live, _ = lax.while_loop(lambda c: jnp.logical_and(c[0] < i, c[2] > 0), full_body,
                                       (jnp.int32(0), early_live, later_live))

    def early_body(carry):
        jj, _ = carry
        return jj + 1, phase(i - 1 - jj, False, True)[0]

    lax.while_loop(lambda c: jnp.logical_and(c[0] < i, c[1] > 0), early_body, (jj, early_live))
    for p in range(n_pairs):
        acc = acc_ref[p]
        o_ref[:, p * LANES:(p + 1) * LANES] = jnp.where(first_head, acc[:blk], acc[blk:]).astype(o_ref.dtype)


def _sb_attention(proj, uu, bsz, seq, n_pairs, pairs_per_step, q_col, k_col, v_col):
    nq = seq // SB_BLOCK
    gp = pairs_per_step
    width = gp * LANES
    return pl.pallas_call(
        functools.partial(_sb_kernel, n_pairs=gp),
        grid=(bsz, n_pairs // gp, nq),
        in_specs=[
            pl.BlockSpec((SB_BLOCK, width), lambda b, p, i: (b * nq + i, q_col + p)),
            pl.BlockSpec((seq, width), lambda b, p, i: (b, k_col + p)),
            pl.BlockSpec((seq, width), lambda b, p, i: (b, v_col + p)),
            _resident((2 * SB_BLOCK, 2 * SB_BLOCK)),
        ],
        out_specs=pl.BlockSpec((SB_BLOCK, width), lambda b, p, i: (b * nq + i, p)),
        out_shape=jax.ShapeDtypeStruct((bsz * seq, n_pairs * LANES), BF16),
        scratch_shapes=[pltpu.VMEM((gp, 2 * SB_BLOCK, LANES), BF16),
                        pltpu.VMEM((gp, 2 * SB_BLOCK, LANES), F32),
                        pltpu.VMEM((gp, 2 * SB_BLOCK, SB_BLOCK), F32)],
        compiler_params=_cparams("parallel", "parallel", "arbitrary"),
        name="sb_attention",
    )(proj, proj, proj, uu)


def _ssd_kernel(xbc_ref, z_ref, dt_ref, dtt_ref, convw_ref, convb_ref, dtb_ref, dtbt_ref,
                alog_ref, alogt_ref, dskip_ref, gnorm_ref, tri_ref, trit_ref, shift_ref,
                o_ref, state_ref, prev_ref, *, inner, n_heads):
    c = pl.program_id(1)
    cl = SSM_CHUNK
    gw = SSM_GROUPS * SSM_STATE
    hpg = n_heads // SSM_GROUPS

    @pl.when(c == 0)
    def _():
        state_ref[...] = jnp.zeros_like(state_ref)
        prev_ref[...] = jnp.zeros_like(prev_ref)

    cur = xbc_ref[...]
    both = jnp.concatenate([prev_ref[...], cur], axis=0)
    prev_ref[...] = cur

    def conv_silu(c0, width):
        cols = slice(c0, c0 + width)
        delayed = jnp.dot(shift_ref[...], both[:, cols], preferred_element_type=F32)
        acc = cur[:, cols].astype(F32) * convw_ref[SSM_CONV - 1:SSM_CONV, cols] + convb_ref[:, cols]
        for k in range(1, SSM_CONV):
            acc = acc + delayed[(k - 1) * cl:k * cl] * convw_ref[SSM_CONV - 1 - k:SSM_CONV - k, cols]
        return acc * _sigmoid(acc)

    bm = conv_silu(inner, gw)
    cm = conv_silu(inner + gw, gw)

    def softplus(v):
        return jnp.maximum(v, 0.0) + jnp.log(1.0 + jnp.exp2(jnp.abs(v) * (-LOG2E)))

    dt_r = softplus(dt_ref[...] + dtb_ref[...])
    dt_t = softplus(dtt_ref[...] + dtbt_ref[...])
    adt_r = dt_r * (-jnp.exp(alog_ref[...]))
    adt_t = dt_t * (-jnp.exp(alogt_ref[...]))
    acs_r = sum(jnp.dot(tri_ref[...], p, preferred_element_type=F32) for p in _split3(adt_r)) * LOG2E
    acs_t = sum(jnp.dot(p, trit_ref[...], preferred_element_type=F32) for p in _split3(adt_t)) * LOG2E
    src_t = acs_t - jnp.log(dt_t) * LOG2E

    lane = lax.broadcasted_iota(jnp.int32, (cl, LANES), 1)
    first_head = lane < SSM_HEAD_DIM
    tl = lax.broadcasted_iota(jnp.int32, (cl, cl), 0)
    ts = lax.broadcasted_iota(jnp.int32, (cl, cl), 1)
    tril = ts <= tl

    gsz = inner // SSM_GROUPS
    for g in range(SSM_GROUPS):
        gcols = slice(g * gsz, (g + 1) * gsz)
        xs = conv_silu(g * gsz, gsz)
        y_pairs = []
        bg = bm[:, g * SSM_STATE:(g + 1) * SSM_STATE]
        cg = cm[:, g * SSM_STATE:(g + 1) * SSM_STATE]
        cb = lax.dot_general(cg.astype(BF16), bg.astype(BF16), (((1,), (1,)), ((), ())),
                             preferred_element_type=F32)
        bgt = bg.T
        for pi in range(hpg // 2):
            h0 = g * hpg + 2 * pi
            lhs_y, lhs_s, cdec = [], [], []
            for h in (h0, h0 + 1):
                a_col = acs_r[:, h:h + 1]
                s_row = src_t[h:h + 1, :]
                a_end = acs_t[h:h + 1, cl - 1:cl]
                m = cb * jnp.exp2(jnp.where(tril, a_col - s_row, -jnp.inf))
                c_scaled = cg * jnp.exp2(a_col)
                lhs_y.append(jnp.concatenate([m.astype(BF16), c_scaled.astype(BF16)], axis=1))
                w_row = jnp.exp2(a_end - s_row)
                lhs_s.append((bgt * w_row).astype(BF16))
                cdec.append(jnp.exp2(a_end))
            cols = slice(h0 * SSM_HEAD_DIM, (h0 + 2) * SSM_HEAD_DIM)
            x_pair = xs[:, 2 * pi * SSM_HEAD_DIM:(2 * pi + 2) * SSM_HEAD_DIM].astype(BF16)
            s_prev = state_ref[:, cols]
            rhs = jnp.concatenate([x_pair, s_prev.astype(BF16)], axis=0)
            out = jnp.dot(jnp.concatenate(lhs_y, axis=0), rhs, preferred_element_type=F32)
            y_pairs.append(jnp.where(first_head, out[:cl], out[cl:]))
            st = jnp.dot(jnp.concatenate(lhs_s, axis=0), x_pair, preferred_element_type=F32)
            st = jnp.where(first_head, st[:SSM_STATE], st[SSM_STATE:])
            state_ref[:, cols] = s_prev * jnp.where(first_head, cdec[0], cdec[1]) + st

        y = jnp.concatenate(y_pairs, axis=1) + dskip_ref[:, gcols] * xs
        zf = z_ref[:, gcols].astype(F32)
        y = y * (zf * _sigmoid(zf))
        y = y * lax.rsqrt(jnp.mean(y * y, axis=-1, keepdims=True) + RMS_EPS)
        o_ref[:, gcols] = (y * gnorm_ref[:, gcols]).astype(o_ref.dtype)


def _ssd(proj, dt, dtt, prm, bsz, seq, inner, n_heads, xbc_col, z_col):
    nc = seq // SSM_CHUNK
    cw = inner + 2 * SSM_GROUPS * SSM_STATE
    const = lambda shape: pl.BlockSpec(shape, lambda b, c: (0, 0))
    kern = functools.partial(_ssd_kernel, inner=inner, n_heads=n_heads)
    return pl.pallas_call(
        kern,
        grid=(bsz, nc),
        in_specs=[
            pl.BlockSpec((SSM_CHUNK, cw), lambda b, c: (b * nc + c, xbc_col)),
            pl.BlockSpec((SSM_CHUNK, inner), lambda b, c: (b * nc + c, z_col)),
            pl.BlockSpec((SSM_CHUNK, LANES), lambda b, c: (b * nc + c, 0)),
            pl.BlockSpec((n_heads, SSM_CHUNK), lambda b, c: (0, b * nc + c)),
            const((SSM_CONV, cw)),
            const((1, cw)),
            const((1, LANES)),
            const((n_heads, SSM_CHUNK)),
            const((1, LANES)),
            const((n_heads, SSM_CHUNK)),
            const((1, inner)),
            const((1, inner)),
            const((SSM_CHUNK, SSM_CHUNK)),
            const((SSM_CHUNK, SSM_CHUNK)),
            const(((SSM_CONV - 1) * SSM_CHUNK, 2 * SSM_CHUNK)),
        ],
        out_specs=pl.BlockSpec((SSM_CHUNK, inner), lambda b, c: (b * nc + c, 0)),
        out_shape=jax.ShapeDtypeStruct((bsz * seq, inner), BF16),
        scratch_shapes=[pltpu.VMEM((SSM_STATE, inner), F32), pltpu.VMEM((SSM_CHUNK, cw), BF16)],
        compiler_params=_cparams("parallel", "arbitrary"),
        name="ssd",
    )(proj, proj, dt, dtt, prm["conv_w"], prm["conv_b"], prm["dt_bias_row"], prm["dt_bias_col"],
      prm["a_log_row"], prm["a_log_col"], prm["d_skip"], prm["g_norm"], prm["tri"], prm["trit"],
      prm["shift"])


def _merge_kernel(x_ref, oa_ref, os_ref, ga_ref, gs_ref, wa_ref, ws_ref, wm_ref, g_ref, o_ref):
    a = jnp.dot(oa_ref[...], wa_ref[...], preferred_element_type=F32)
    s = jnp.dot(os_ref[...], ws_ref[...], preferred_element_type=F32)
    merged = _sigmoid(ga_ref[...].astype(F32)) * a + _sigmoid(gs_ref[...].astype(F32)) * s
    y = jnp.dot(merged.astype(BF16), wm_ref[...], preferred_element_type=F32)
    o_ref[...] = x_ref[...] + _rms(y, g_ref[...])


def _merge(x2, o_att, o_ssm, proj, wa, ws, wm, g, layer, tm, ga_col, gs_col):
    t, d = x2.shape
    inner = o_ssm.shape[1]
    sbw = o_att.shape[1]
    return pl.pallas_call(
        _merge_kernel,
        grid=(t // tm,),
        in_specs=[
            pl.BlockSpec((tm, d), lambda i: (i, 0)),
            pl.BlockSpec((tm, sbw), lambda i: (i, 0)),
            pl.BlockSpec((tm, inner), lambda i: (i, 0)),
            pl.BlockSpec((tm, d), lambda i: (i, ga_col)),
            pl.BlockSpec((tm, d), lambda i: (i, gs_col)),
            _layer_resident((sbw, d), layer),
            _layer_resident((inner, d), layer),
            _layer_resident((d, d), layer),
            _resident((1, d)),
        ],
        out_specs=pl.BlockSpec((tm, d), lambda i: (i, 0)),
        out_shape=jax.ShapeDtypeStruct((t, d), F32),
        compiler_params=_cparams("parallel"),
        name="merge",
    )(x2, o_att, o_ssm, proj, proj, wa, ws, wm, g)


def _xattn_kernel(x_ref, kv_ref, gpre_ref, wq_ref, wo_ref, gpost_ref, o_ref, *, d):
    x = x_ref[...]
    h = _rms(x, gpre_ref[...]).astype(BF16)
    hd = d // XA_HEADS
    q = (jnp.dot(h, wq_ref[...], preferred_element_type=F32) * (hd ** -0.5)).astype(BF16)
    outs = []
    for n in range(XA_HEADS):
        kh = kv_ref[:, n * hd:(n + 1) * hd]
        vh = kv_ref[:, d + n * hd:d + (n + 1) * hd]
        s = lax.dot_general(q[:, n * hd:(n + 1) * hd], kh, (((1,), (1,)), ((), ())),
                            preferred_element_type=F32)
        p = jnp.exp(s - jnp.max(s, axis=-1, keepdims=True))
        denom = jnp.sum(p, axis=-1, keepdims=True)
        o = jnp.dot(p.astype(BF16), vh, preferred_element_type=F32)
        outs.append((o / denom).astype(BF16))
    o = jnp.concatenate(outs, axis=1)
    y = jnp.dot(o, wo_ref[...], preferred_element_type=F32)
    o_ref[...] = x + _rms(y, gpost_ref[...])


def _xattn(x2, kv, gpre, wq, wo, gpost, layer, tm, seq, mem_len):
    t, d = x2.shape
    per_b = seq // tm
    return pl.pallas_call(
        functools.partial(_xattn_kernel, d=d),
        grid=(t // tm,),
        in_specs=[
            pl.BlockSpec((tm, d), lambda i: (i, 0)),
            pl.BlockSpec((mem_len, 2 * d), lambda i: (i // per_b, 0)),
            _resident((1, d)),
            _layer_resident((d, d), layer),
            _layer_resident((d, d), layer),
            _resident((1, d)),
        ],
        out_specs=pl.BlockSpec((tm, d), lambda i: (i, 0)),
        out_shape=jax.ShapeDtypeStruct((t, d), F32),
        compiler_params=_cparams("parallel"),
        name="xattn",
    )(x2, kv, gpre, wq, wo, gpost)


def _ffn_kernel(x_ref, gpre_ref, wgu_ref, wd_ref, gpost_ref, o_ref, *, hidden, chunk):
    x = x_ref[...]
    h = _rms(x, gpre_ref[...]).astype(BF16)
    y = jnp.zeros(x.shape, F32)
    for c0 in range(0, hidden, chunk):
        gate = jnp.dot(h, wgu_ref[:, c0:c0 + chunk], preferred_element_type=F32)
        up = jnp.dot(h, wgu_ref[:, hidden + c0:hidden + c0 + chunk], preferred_element_type=F32)
        act = (gate * _sigmoid(gate) * up).astype(BF16)
        y = y + jnp.dot(act, wd_ref[c0:c0 + chunk, :], preferred_element_type=F32)
    o_ref[...] = x + _rms(y, gpost_ref[...])


def _ffn(x2, gpre, wgu, wd, gpost, layer, tm):
    t, d = x2.shape
    hidden = wd.shape[1]
    chunk = hidden
    for cand in (512, 384, 256, 128):
        if hidden % cand == 0:
            chunk = cand
            break
    return pl.pallas_call(
        functools.partial(_ffn_kernel, hidden=hidden, chunk=chunk),
        grid=(t // tm,),
        in_specs=[
            pl.BlockSpec((tm, d), lambda i: (i, 0)),
            _resident((1, d)),
            _layer_resident((d, 2 * hidden), layer),
            _layer_resident((hidden, d), layer),
            _resident((1, d)),
        ],
        out_specs=pl.BlockSpec((tm, d), lambda i: (i, 0)),
        out_shape=jax.ShapeDtypeStruct((t, d), F32),
        compiler_params=_cparams("parallel"),
        name="ffn",
    )(x2, gpre, wgu, wd, gpost)


def _segment_layout(widths, tile):
    offs, cur = [], 0
    for w in widths:
        cur = -(-cur // w) * w
        offs.append(cur)
        cur += w
    return offs, -(-cur // tile) * tile


def kernel(x, mem, g_pre_mix, w_in, conv_w, conv_b, dt_bias, a_log, d_skip, g_ssm_norm, w_br_att,
           w_br_ssm, w_mix_out, g_post_mix, g_pre_xa, g_mem, w_xq, w_xkv, w_xo, g_post_xa,
           g_pre_ffn, w_gu, w_down, g_post_ffn):
    bsz, seq, d = x.shape
    depth = w_in.shape[0]
    mem_len = mem.shape[1]
    sbw = w_br_att.shape[1]
    inner = w_br_ssm.shape[1]
    n_heads = inner // SSM_HEAD_DIM
    gw = SSM_GROUPS * SSM_STATE
    cw = inner + 2 * gw
    t = bsz * seq
    assert seq % SB_BLOCK == 0 and seq % SSM_CHUNK == 0 and sbw % LANES == 0
    assert n_heads % (2 * SSM_GROUPS) == 0 and n_heads <= LANES and (d // XA_HEADS) % LANES == 0

    src = {}
    cur = 0
    for name, wd_ in (("q", sbw), ("k", sbw), ("v", sbw), ("z", inner), ("xbc", cw),
                      ("dt", n_heads), ("ga", d), ("gs", d)):
        src[name] = (cur, wd_)
        cur += wd_
    order = ("q", "k", "v", "xbc", "z", "ga", "gs")
    tn = 1024
    offs, n_total = _segment_layout([src[k][1] for k in order], tn)
    col = dict(zip(order, offs))

    tm_proj = _pick_tile(t, 2048)
    tm_row = _pick_tile(seq, 512)

    ti = lax.broadcasted_iota(jnp.int32, (SSM_CHUNK, SSM_CHUNK), 0)
    tj = lax.broadcasted_iota(jnp.int32, (SSM_CHUNK, SSM_CHUNK), 1)
    tri = (tj <= ti).astype(BF16)
    trit = (ti <= tj).astype(BF16)
    suffix = (ti >= tj).astype(BF16)
    half = jnp.concatenate([suffix, jnp.ones((SB_BLOCK, SB_BLOCK), BF16)], axis=1)
    uu = jnp.concatenate([half, half], axis=0)
    si = lax.broadcasted_iota(jnp.int32, (SSM_CHUNK, 2 * SSM_CHUNK), 0)
    sj = lax.broadcasted_iota(jnp.int32, (SSM_CHUNK, 2 * SSM_CHUNK), 1)
    shift = jnp.concatenate([(sj == si + SSM_CHUNK - k).astype(BF16) for k in range(1, SSM_CONV)],
                            axis=0)

    x2 = x.reshape(t, d)
    mem2 = mem.reshape(bsz * mem_len, d)
    row = lambda v: v.reshape(1, -1)

    pieces, cur = [], 0
    for k in order:
        s0, wd_ = src[k]
        if col[k] > cur:
            pieces.append(jnp.zeros((depth, d, col[k] - cur), BF16))
        pieces.append(w_in[:, :, s0:s0 + wd_].astype(BF16))
        cur = col[k] + wd_
    if n_total > cur:
        pieces.append(jnp.zeros((depth, d, n_total - cur), BF16))
    w_perm = jnp.concatenate(pieces, axis=2)
    s0, wd_ = src["dt"]
    w_dt = w_in[:, :, s0:s0 + wd_].astype(BF16)
    w_dt_pad = jnp.concatenate([w_dt, jnp.zeros((depth, d, LANES - n_heads), BF16)], axis=2)
    wb = {name: w.astype(BF16) for name, w in (
        ("br_att", w_br_att), ("br_ssm", w_br_ssm), ("mix_out", w_mix_out), ("xkv", w_xkv),
        ("xq", w_xq), ("xo", w_xo), ("gu", w_gu), ("down", w_down))}

    for l in range(depth):
        proj, dt, dtt = _inproj(x2, row(g_pre_mix[l]), w_perm, w_dt_pad, n_heads, l, tm_proj, tn)

        n_pairs = sbw // LANES
        gp = SB_PAIRS_PER_STEP if n_pairs % SB_PAIRS_PER_STEP == 0 else 1
        o_att = _sb_attention(proj, uu, bsz, seq, n_pairs, gp, col["q"] // (gp * LANES),
                              col["k"] // (gp * LANES), col["v"] // (gp * LANES))
        pad_row = lambda v: jnp.zeros((1, LANES), F32).at[0, :n_heads].set(v)
        prm = {
            "conv_w": conv_w[l], "conv_b": row(conv_b[l]),
            "dt_bias_row": pad_row(dt_bias[l]),
            "dt_bias_col": jnp.broadcast_to(dt_bias[l][:, None], (n_heads, SSM_CHUNK)),
            "a_log_row": pad_row(a_log[l]),
            "a_log_col": jnp.broadcast_to(a_log[l][:, None], (n_heads, SSM_CHUNK)),
            "d_skip": row(jnp.repeat(d_skip[l], SSM_HEAD_DIM)),
            "g_norm": row(g_ssm_norm[l]),
            "tri": tri, "trit": trit, "shift": shift,
        }
        o_ssm = _ssd(proj, dt, dtt, prm, bsz, seq, inner, n_heads, col["xbc"] // cw, col["z"] // inner)
        x2 = _merge(x2, o_att, o_ssm, proj, wb["br_att"], wb["br_ssm"], wb["mix_out"],
                    row(g_post_mix[l]), l, tm_row, col["ga"] // d, col["gs"] // d)

        kv = _norm_matmul(mem2, row(g_mem[l]), wb["xkv"], l, _pick_tile(bsz * mem_len, 512))
        x2 = _xattn(x2, kv, row(g_pre_xa[l]), wb["xq"], wb["xo"], row(g_post_xa[l]), l, tm_row,
                    seq, mem_len)

        x2 = _ffn(x2, row(g_pre_ffn[l]), wb["gu"], wb["down"], row(g_post_ffn[l]), l, tm_row)
    return x2.reshape(bsz, seq, d)
```

```python
import functools

import jax
import jax.numpy as jnp
from jax import lax
from jax.experimental import pallas as pl
from jax.experimental.pallas import tpu as pltpu

F32 = jnp.float32
BF16 = jnp.bfloat16

RMS_EPS = 1e-6
LOG2E = 1.4426950408889634
SB_HEAD_DIM = 64
SB_BLOCK = 128
SB_PAIRS_PER_STEP = 8
SB_DEAD_LOG = -105.0
SB_EARLY_ROWS = 64
SSM_HEAD_DIM = 64
SSM_GROUPS = 4
SSM_STATE = 128
SSM_CONV = 4
SSM_CHUNK = 128
XA_HEADS = 4
LANES = 128
VMEM_LIMIT_BYTES = 56 * 1024 * 1024


def _cparams(*sem):
    return pltpu.CompilerParams(dimension_semantics=sem, vmem_limit_bytes=VMEM_LIMIT_BYTES)


def _pick_tile(n, want):
    t = min(n, want)
    while n % t:
        t //= 2
    return t


def _resident(shape):
    return pl.BlockSpec(shape, lambda *_: (0,) * len(shape), pipeline_mode=pl.Buffered(1))


def _layer_resident(shape, layer):
    return pl.BlockSpec((None,) + tuple(shape), lambda *_: (layer,) + (0,) * len(shape),
                        pipeline_mode=pl.Buffered(1))


def _rms(x, g):
    ms = jnp.mean(x * x, axis=-1, keepdims=True)
    return x * lax.rsqrt(ms + RMS_EPS) * g


def _sigmoid(x):
    return 1.0 / (1.0 + jnp.exp2(x * (-LOG2E)))


def _split3(x):
    hi = x.astype(BF16)
    r = x - hi.astype(F32)
    mid = r.astype(BF16)
    lo = (r - mid.astype(F32)).astype(BF16)
    return hi, mid, lo


def _inproj_kernel(x_ref, g_ref, w_ref, wdt_ref, o_ref, dt_ref, dtt_ref, h_ref):
    @pl.when(pl.program_id(1) == 0)
    def _():
        h = _rms(x_ref[...], g_ref[...]).astype(BF16)
        h_ref[...] = h
        dt = jnp.dot(h, wdt_ref[...], preferred_element_type=F32)
        dt_ref[...] = dt
        dtt_ref[...] = dt.T[:dtt_ref.shape[0], :]

    o_ref[...] = jnp.dot(h_ref[...], w_ref[...], preferred_element_type=F32).astype(o_ref.dtype)


def _inproj(x2, g, w, wdt, nh, layer, tm, tn):
    t, d = x2.shape
    n = w.shape[2]
    return pl.pallas_call(
        _inproj_kernel,
        grid=(t // tm, n // tn),
        in_specs=[
            pl.BlockSpec((tm, d), lambda i, j: (i, 0)),
            pl.BlockSpec((1, d), lambda i, j: (0, 0)),
            pl.BlockSpec((None, d, tn), lambda i, j: (layer, 0, j)),
            pl.BlockSpec((None, d, LANES), lambda i, j: (layer, 0, 0)),
        ],
        out_specs=[
            pl.BlockSpec((tm, tn), lambda i, j: (i, j)),
            pl.BlockSpec((tm, LANES), lambda i, j: (i, 0)),
            pl.BlockSpec((nh, tm), lambda i, j: (0, i)),
        ],
        out_shape=[
            jax.ShapeDtypeStruct((t, n), BF16),
            jax.ShapeDtypeStruct((t, LANES), F32),
            jax.ShapeDtypeStruct((nh, t), F32),
        ],
        scratch_shapes=[pltpu.VMEM((tm, d), BF16)],
        compiler_params=_cparams("parallel", "arbitrary"),
        name="inproj",
    )(x2, g, w, wdt)


def _norm_matmul_kernel(x_ref, g_ref, w_ref, o_ref):
    h = _rms(x_ref[...], g_ref[...]).astype(BF16)
    o_ref[...] = jnp.dot(h, w_ref[...], preferred_element_type=F32).astype(o_ref.dtype)


def _norm_matmul(x2, g, w, layer, tm):
    t, d = x2.shape
    n = w.shape[2]
    return pl.pallas_call(
        _norm_matmul_kernel,
        grid=(t // tm,),
        in_specs=[
            pl.BlockSpec((tm, d), lambda i: (i, 0)),
            pl.BlockSpec((1, d), lambda i: (0, 0)),
            _layer_resident((d, n), layer),
        ],
        out_specs=pl.BlockSpec((tm, n), lambda i: (i, 0)),
        out_shape=jax.ShapeDtypeStruct((t, n), BF16),
        compiler_params=_cparams("parallel"),
        name="mem_kv",
    )(x2, g, w)


def _sb_kernel(q_ref, k_ref, v_ref, uu_ref, o_ref, qs_ref, acc_ref, run_ref, *, n_pairs):
    i = pl.program_id(2)
    blk = SB_BLOCK
    lane = lax.broadcasted_iota(jnp.int32, (blk, LANES), 1)
    first_head = lane < SB_HEAD_DIM
    row = lax.broadcasted_iota(jnp.int32, (2 * blk, blk), 0)
    col = lax.broadcasted_iota(jnp.int32, (2 * blk, blk), 1)
    causal = col < jnp.where(row >= blk, row - blk, row)

    for p in range(n_pairs):
        q = q_ref[:, p * LANES:(p + 1) * LANES] * (-(SB_HEAD_DIM ** -0.5))
        zero = jnp.zeros_like(q)
        qs_ref[p] = jnp.concatenate([jnp.where(first_head, q, zero), jnp.where(first_head, zero, q)],
                                    axis=0)

    hr = SB_EARLY_ROWS

    def take(ref, p, early_only):
        if not early_only:
            return ref[p]
        return jnp.concatenate([ref[p, 0:hr, :], ref[p, blk:blk + hr, :]], axis=0)

    def put(ref, p, val, early_only):
        if not early_only:
            ref[p] = val
        else:
            ref[p, 0:hr, :] = val[:hr]
            ref[p, blk:blk + hr, :] = val[hr:]

    def any_live(run_max):
        return (jnp.max(run_max) > SB_DEAD_LOG).astype(jnp.int32)

    def phase(j, diag, early_only):
        pairs = range(n_pairs)
        start = pl.multiple_of(j * blk, blk)
        zns = [lax.dot_general(take(qs_ref, p, early_only),
                               k_ref[pl.ds(start, blk), p * LANES:(p + 1) * LANES],
                               (((1,), (1,)), ((), ())), preferred_element_type=F32) for p in pairs]
        cs = []
        for p in pairs:
            zn = zns[p]
            log_1mb = jnp.minimum(zn, 0.0) - jnp.log(1.0 + jnp.exp2(jnp.abs(zn) * (-LOG2E)))
            if diag:
                log_1mb = jnp.where(causal, log_1mb, 0.0)
            hi = log_1mb.astype(BF16)
            lo = (log_1mb - hi.astype(F32)).astype(BF16)
            cs.append(jnp.dot(jnp.concatenate([hi, lo], axis=1), uu_ref[...],
                              preferred_element_type=F32))
        ws = []
        run_max = None
        for p in pairs:
            c = cs[p]
            if diag:
                tail = c[:, :blk]
                run = c[:, blk:]
            else:
                run = take(run_ref, p, early_only)
                tail = c[:, :blk] + run
                run = run + c[:, blk:]
            put(run_ref, p, run, early_only)
            run_max = run if run_max is None else jnp.maximum(run_max, run)
            w = jnp.exp2((tail - zns[p]) * LOG2E)
            if diag:
                w = jnp.where(causal, w, 0.0)
            ws.append(w.astype(BF16))
        for p in pairs:
            pv = jnp.dot(ws[p], v_ref[pl.ds(start, blk), p * LANES:(p + 1) * LANES],
                         preferred_element_type=F32)
            if diag:
                acc_ref[p] = pv
            else:
                put(acc_ref, p, take(acc_ref, p, early_only) + pv, early_only)
        if early_only:
            return any_live(run_max), jnp.int32(0)
        early = jnp.maximum(run_max[0:hr], run_max[blk:blk + hr])
        later = jnp.maximum(run_max[hr:blk], run_max[blk + hr:])
        return any_live(early), any_live(later)

    early_live, later_live = phase(i, True, False)

    def full_body(carry):
        jj, _, _ = carry
        return (jj + 1,) + phase(i - 1 - jj, False, False)

    jj, early_live, _ = lax.while_loop(lambda c: jnp.logical_and(c[0] < i, c[2] > 0), full_body,
                                       (jnp.int32(0), early_live, later_live))

    def early_body(carry):
        jj, _ = carry
        return jj + 1, phase(i - 1 - jj, False, True)[0]

    lax.while_loop(lambda c: jnp.logical_and(c[0] < i, c[1] > 0), early_body, (jj, early_live))
    for p in range(n_pairs):
        acc = acc_ref[p]
        o_ref[:, p * LANES:(p + 1) * LANES] = jnp.where(first_head, acc[:blk], acc[blk:]).astype(o_ref.dtype)


def _sb_attention(proj, uu, bsz, seq, n_pairs, pairs_per_step, q_col, k_col, v_col):
    nq = seq // SB_BLOCK
    gp = pairs_per_step
    width = gp * LANES
    return pl.pallas_call(
        functools.partial(_sb_kernel, n_pairs=gp),
        grid=(bsz, n_pairs // gp, nq),
        in_specs=[
            pl.BlockSpec((SB_BLOCK, width), lambda b, p, i: (b * nq + i, q_col + p)),
            pl.BlockSpec((seq, width), lambda b, p, i: (b, k_col + p)),
            pl.BlockSpec((seq, width), lambda b, p, i: (b, v_col + p)),
            _resident((2 * SB_BLOCK, 2 * SB_BLOCK)),
        ],
        out_specs=pl.BlockSpec((SB_BLOCK, width), lambda b, p, i: (b * nq + i, p)),
        out_shape=jax.ShapeDtypeStruct((bsz * seq, n_pairs * LANES), BF16),
        scratch_shapes=[pltpu.VMEM((gp, 2 * SB_BLOCK, LANES), BF16),
                        pltpu.VMEM((gp, 2 * SB_BLOCK, LANES), F32),
                        pltpu.VMEM((gp, 2 * SB_BLOCK, SB_BLOCK), F32)],
        compiler_params=_cparams("parallel", "parallel", "arbitrary"),
        name="sb_attention",
    )(proj, proj, proj, uu)


def _ssd_kernel(xbc_ref, z_ref, dt_ref, dtt_ref, convw_ref, convb_ref, dtb_ref, dtbt_ref,
                alog_ref, alogt_ref, dskip_ref, gnorm_ref, tri_ref, trit_ref, shift_ref,
                o_ref, state_ref, prev_ref, *, inner, n_heads):
    c = pl.program_id(1)
    cl = SSM_CHUNK
    gw = SSM_GROUPS * SSM_STATE
    hpg = n_heads // SSM_GROUPS

    @pl.when(c == 0)
    def _():
        state_ref[...] = jnp.zeros_like(state_ref)
        prev_ref[...] = jnp.zeros_like(prev_ref)

    cur = xbc_ref[...]
    halo = prev_ref[...]
    prev_ref[...] = cur[cl - 8:, :].astype(F32)

    def conv_silu(c0, width):
        cols = slice(c0, c0 + width)
        delayed = jnp.dot(shift_ref[...], cur[:, cols], preferred_element_type=F32)
        acc = cur[:, cols].astype(F32) * convw_ref[SSM_CONV - 1:SSM_CONV, cols] + convb_ref[:, cols]
        for k in range(1, SSM_CONV):
            d_k = delayed[(k - 1) * cl:k * cl]
            edge = lax.broadcasted_iota(jnp.int32, (8, width), 0) < k
            top = jnp.where(edge, pltpu.roll(halo[:, cols], k, 0), d_k[:8])
            d_k = jnp.concatenate([top, d_k[8:]], axis=0)
            acc = acc + d_k * convw_ref[SSM_CONV - 1 - k:SSM_CONV - k, cols]
        return acc * _sigmoid(acc)

    bm = conv_silu(inner, gw)
    cm = conv_silu(inner + gw, gw)

    def softplus(v):
        return jnp.maximum(v, 0.0) + jnp.log(1.0 + jnp.exp2(jnp.abs(v) * (-LOG2E)))

    dt_r = softplus(dt_ref[...] + dtb_ref[...])
    dt_t = softplus(dtt_ref[...] + dtbt_ref[...])
    adt_r = dt_r * (-jnp.exp(alog_ref[...]))
    adt_t = dt_t * (-jnp.exp(alogt_ref[...]))
    acs_r = sum(jnp.dot(tri_ref[...], p, preferred_element_type=F32) for p in _split3(adt_r)) * LOG2E
    acs_t = sum(jnp.dot(p, trit_ref[...], preferred_element_type=F32) for p in _split3(adt_t)) * LOG2E
    src_t = acs_t - jnp.log(dt_t) * LOG2E

    lane = lax.broadcasted_iota(jnp.int32, (cl, LANES), 1)
    first_head = lane < SSM_HEAD_DIM
    tl = lax.broadcasted_iota(jnp.int32, (cl, cl), 0)
    ts = lax.broadcasted_iota(jnp.int32, (cl, cl), 1)
    tril = ts <= tl

    gsz = inner // SSM_GROUPS
    for g in range(SSM_GROUPS):
        gcols = slice(g * gsz, (g + 1) * gsz)
        xs = conv_silu(g * gsz, gsz)
        y_pairs = []
        bg = bm[:, g * SSM_STATE:(g + 1) * SSM_STATE]
        cg = cm[:, g * SSM_STATE:(g + 1) * SSM_STATE]
        cb = lax.dot_general(cg.astype(BF16), bg.astype(BF16), (((1,), (1,)), ((), ())),
                             preferred_element_type=F32)
        bgt = bg.T
        for pi in range(hpg // 2):
            h0 = g * hpg + 2 * pi
            lhs_y, lhs_s, cdec = [], [], []
            for h in (h0, h0 + 1):
                a_col = acs_r[:, h:h + 1]
                s_row = src_t[h:h + 1, :]
                a_end = acs_t[h:h + 1, cl - 1:cl]
                m = cb * jnp.exp2(jnp.where(tril, a_col - s_row, -jnp.inf))
                c_scaled = cg * jnp.exp2(a_col)
                lhs_y.append(jnp.concatenate([m.astype(BF16), c_scaled.astype(BF16)], axis=1))
                w_row = jnp.exp2(a_end - s_row)
                lhs_s.append((bgt * w_row).astype(BF16))
                cdec.append(jnp.exp2(a_end))
            cols = slice(h0 * SSM_HEAD_DIM, (h0 + 2) * SSM_HEAD_DIM)
            x_pair = xs[:, 2 * pi * SSM_HEAD_DIM:(2 * pi + 2) * SSM_HEAD_DIM].astype(BF16)
            s_prev = state_ref[:, cols]
            rhs = jnp.concatenate([x_pair, s_prev.astype(BF16)], axis=0)
            out = jnp.dot(jnp.concatenate(lhs_y, axis=0), rhs, preferred_element_type=F32)
            y_pairs.append(jnp.where(first_head, out[:cl], out[cl:]))
            st = jnp.dot(jnp.concatenate(lhs_s, axis=0), x_pair, preferred_element_type=F32)
            st = jnp.where(first_head, st[:SSM_STATE], st[SSM_STATE:])
            state_ref[:, cols] = s_prev * jnp.where(first_head, cdec[0], cdec[1]) + st

        y = jnp.concatenate(y_pairs, axis=1) + dskip_ref[:, gcols] * xs
        zf = z_ref[:, gcols].astype(F32)
        y = y * (zf * _sigmoid(zf))
        y = y * lax.rsqrt(jnp.mean(y * y, axis=-1, keepdims=True) + RMS_EPS)
        o_ref[:, gcols] = (y * gnorm_ref[:, gcols]).astype(o_ref.dtype)


def _ssd(proj, dt, dtt, prm, bsz, seq, inner, n_heads, xbc_col, z_col):
    nc = seq // SSM_CHUNK
    cw = inner + 2 * SSM_GROUPS * SSM_STATE
    const = lambda shape: pl.BlockSpec(shape, lambda b, c: (0, 0))
    kern = functools.partial(_ssd_kernel, inner=inner, n_heads=n_heads)
    return pl.pallas_call(
        kern,
        grid=(bsz, nc),
        in_specs=[
            pl.BlockSpec((SSM_CHUNK, cw), lambda b, c: (b * nc + c, xbc_col)),
            pl.BlockSpec((SSM_CHUNK, inner), lambda b, c: (b * nc + c, z_col)),
            pl.BlockSpec((SSM_CHUNK, LANES), lambda b, c: (b * nc + c, 0)),
            pl.BlockSpec((n_heads, SSM_CHUNK), lambda b, c: (0, b * nc + c)),
            const((SSM_CONV, cw)),
            const((1, cw)),
            const((1, LANES)),
            const((n_heads, SSM_CHUNK)),
            const((1, LANES)),
            const((n_heads, SSM_CHUNK)),
            const((1, inner)),
            const((1, inner)),
            const((SSM_CHUNK, SSM_CHUNK)),
            const((SSM_CHUNK, SSM_CHUNK)),
            const(((SSM_CONV - 1) * SSM_CHUNK, SSM_CHUNK)),
        ],
        out_specs=pl.BlockSpec((SSM_CHUNK, inner), lambda b, c: (b * nc + c, 0)),
        out_shape=jax.ShapeDtypeStruct((bsz * seq, inner), BF16),
        scratch_shapes=[pltpu.VMEM((SSM_STATE, inner), F32), pltpu.VMEM((8, cw), F32)],
        compiler_params=_cparams("parallel", "arbitrary"),
        name="ssd",
    )(proj, proj, dt, dtt, prm["conv_w"], prm["conv_b"], prm["dt_bias_row"], prm["dt_bias_col"],
      prm["a_log_row"], prm["a_log_col"], prm["d_skip"], prm["g_norm"], prm["tri"], prm["trit"],
      prm["shift"])


def _merge_kernel(x_ref, oa_ref, os_ref, ga_ref, gs_ref, wa_ref, ws_ref, wm_ref, g_ref, o_ref):
    a = jnp.dot(oa_ref[...], wa_ref[...], preferred_element_type=F32)
    s = jnp.dot(os_ref[...], ws_ref[...], preferred_element_type=F32)
    merged = _sigmoid(ga_ref[...].astype(F32)) * a + _sigmoid(gs_ref[...].astype(F32)) * s
    y = jnp.dot(merged.astype(BF16), wm_ref[...], preferred_element_type=F32)
    o_ref[...] = x_ref[...] + _rms(y, g_ref[...])


def _merge(x2, o_att, o_ssm, proj, wa, ws, wm, g, layer, tm, ga_col, gs_col):
    t, d = x2.shape
    inner = o_ssm.shape[1]
    sbw = o_att.shape[1]
    return pl.pallas_call(
        _merge_kernel,
        grid=(t // tm,),
        in_specs=[
            pl.BlockSpec((tm, d), lambda i: (i, 0)),
            pl.BlockSpec((tm, sbw), lambda i: (i, 0)),
            pl.BlockSpec((tm, inner), lambda i: (i, 0)),
            pl.BlockSpec((tm, d), lambda i: (i, ga_col)),
            pl.BlockSpec((tm, d), lambda i: (i, gs_col)),
            _layer_resident((sbw, d), layer),
            _layer_resident((inner, d), layer),
            _layer_resident((d, d), layer),
            _resident((1, d)),
        ],
        out_specs=pl.BlockSpec((tm, d), lambda i: (i, 0)),
        out_shape=jax.ShapeDtypeStruct((t, d), F32),
        compiler_params=_cparams("parallel"),
        name="merge",
    )(x2, o_att, o_ssm, proj, proj, wa, ws, wm, g)


def _xattn_kernel(x_ref, kv_ref, gpre_ref, wq_ref, wo_ref, gpost_ref, o_ref, *, d):
    x = x_ref[...]
    h = _rms(x, gpre_ref[...]).astype(BF16)
    hd = d // XA_HEADS
    q = (jnp.dot(h, wq_ref[...], preferred_element_type=F32) * (hd ** -0.5)).astype(BF16)
    heads = range(XA_HEADS)
    scores = [lax.dot_general(q[:, n * hd:(n + 1) * hd], kv_ref[:, n * hd:(n + 1) * hd],
                              (((1,), (1,)), ((), ())), preferred_element_type=F32)
              for n in heads]
    probs, denoms = [], []
    for n in heads:
        p = jnp.exp(scores[n] - jnp.max(scores[n], axis=-1, keepdims=True))
        denoms.append(jnp.sum(p, axis=-1, keepdims=True))
        probs.append(p.astype(BF16))
    outs = [(jnp.dot(probs[n], kv_ref[:, d + n * hd:d + (n + 1) * hd], preferred_element_type=F32)
             / denoms[n]).astype(BF16) for n in heads]
    o = jnp.concatenate(outs, axis=1)
    y = jnp.dot(o, wo_ref[...], preferred_element_type=F32)
    o_ref[...] = x + _rms(y, gpost_ref[...])


def _xattn(x2, kv, gpre, wq, wo, gpost, layer, tm, seq, mem_len):
    t, d = x2.shape
    per_b = seq // tm
    return pl.pallas_call(
        functools.partial(_xattn_kernel, d=d),
        grid=(t // tm,),
        in_specs=[
            pl.BlockSpec((tm, d), lambda i: (i, 0)),
            pl.BlockSpec((mem_len, 2 * d), lambda i: (i // per_b, 0)),
            _resident((1, d)),
            _layer_resident((d, d), layer),
            _layer_resident((d, d), layer),
            _resident((1, d)),
        ],
        out_specs=pl.BlockSpec((tm, d), lambda i: (i, 0)),
        out_shape=jax.ShapeDtypeStruct((t, d), F32),
        compiler_params=_cparams("parallel"),
        name="xattn",
    )(x2, kv, gpre, wq, wo, gpost)


def _ffn_kernel(x_ref, gpre_ref, wgu_ref, wd_ref, gpost_ref, o_ref, *, hidden, chunk):
    x = x_ref[...]
    h = _rms(x, gpre_ref[...]).astype(BF16)
    y = jnp.zeros(x.shape, F32)
    for c0 in range(0, hidden, chunk):
        gate = jnp.dot(h, wgu_ref[:, c0:c0 + chunk], preferred_element_type=F32)
        up = jnp.dot(h, wgu_ref[:, hidden + c0:hidden + c0 + chunk], preferred_element_type=F32)
        act = (gate * _sigmoid(gate) * up).astype(BF16)
        y = y + jnp.dot(act, wd_ref[c0:c0 + chunk, :], preferred_element_type=F32)
    o_ref[...] = x + _rms(y, gpost_ref[...])


def _ffn(x2, gpre, wgu, wd, gpost, layer, tm):
    t, d = x2.shape
    hidden = wd.shape[1]
    chunk = hidden
    for cand in (512, 384, 256, 128):
        if hidden % cand == 0:
            chunk = cand
            break
    return pl.pallas_call(
        functools.partial(_ffn_kernel, hidden=hidden, chunk=chunk),
        grid=(t // tm,),
        in_specs=[
            pl.BlockSpec((tm, d), lambda i: (i, 0)),
            _resident((1, d)),
            _layer_resident((d, 2 * hidden), layer),
            _layer_resident((hidden, d), layer),
            _resident((1, d)),
        ],
        out_specs=pl.BlockSpec((tm, d), lambda i: (i, 0)),
        out_shape=jax.ShapeDtypeStruct((t, d), F32),
        compiler_params=_cparams("parallel"),
        name="ffn",
    )(x2, gpre, wgu, wd, gpost)


def _segment_layout(widths, tile):
    offs, cur = [], 0
    for w in widths:
        cur = -(-cur // w) * w
        offs.append(cur)
        cur += w
    return offs, -(-cur // tile) * tile


def kernel(x, mem, g_pre_mix, w_in, conv_w, conv_b, dt_bias, a_log, d_skip, g_ssm_norm, w_br_att,
           w_br_ssm, w_mix_out, g_post_mix, g_pre_xa, g_mem, w_xq, w_xkv, w_xo, g_post_xa,
           g_pre_ffn, w_gu, w_down, g_post_ffn):
    bsz, seq, d = x.shape
    depth = w_in.shape[0]
    mem_len = mem.shape[1]
    sbw = w_br_att.shape[1]
    inner = w_br_ssm.shape[1]
    n_heads = inner // SSM_HEAD_DIM
    gw = SSM_GROUPS * SSM_STATE
    cw = inner + 2 * gw
    t = bsz * seq
    assert seq % SB_BLOCK == 0 and seq % SSM_CHUNK == 0 and sbw % LANES == 0
    assert n_heads % (2 * SSM_GROUPS) == 0 and n_heads <= LANES and (d // XA_HEADS) % LANES == 0

    src = {}
    cur = 0
    for name, wd_ in (("q", sbw), ("k", sbw), ("v", sbw), ("z", inner), ("xbc", cw),
                      ("dt", n_heads), ("ga", d), ("gs", d)):
        src[name] = (cur, wd_)
        cur += wd_
    order = ("q", "k", "v", "xbc", "z", "ga", "gs")
    tn = 1024
    offs, n_total = _segment_layout([src[k][1] for k in order], tn)
    col = dict(zip(order, offs))

    tm_proj = _pick_tile(t, 2048)
    tm_row = _pick_tile(seq, 512)

    ti = lax.broadcasted_iota(jnp.int32, (SSM_CHUNK, SSM_CHUNK), 0)
    tj = lax.broadcasted_iota(jnp.int32, (SSM_CHUNK, SSM_CHUNK), 1)
    tri = (tj <= ti).astype(BF16)
    trit = (ti <= tj).astype(BF16)
    suffix = (ti >= tj).astype(BF16)
    half = jnp.concatenate([suffix, jnp.ones((SB_BLOCK, SB_BLOCK), BF16)], axis=1)
    uu = jnp.concatenate([half, half], axis=0)
    shift = jnp.concatenate([(tj == ti - k).astype(BF16) for k in range(1, SSM_CONV)], axis=0)

    x2 = x.reshape(t, d)
    mem2 = mem.reshape(bsz * mem_len, d)
    row = lambda v: v.reshape(1, -1)

    pieces, cur = [], 0
    for k in order:
        s0, wd_ = src[k]
        if col[k] > cur:
            pieces.append(jnp.zeros((depth, d, col[k] - cur), BF16))
        pieces.append(w_in[:, :, s0:s0 + wd_].astype(BF16))
        cur = col[k] + wd_
    if n_total > cur:
        pieces.append(jnp.zeros((depth, d, n_total - cur), BF16))
    w_perm = jnp.concatenate(pieces, axis=2)
    s0, wd_ = src["dt"]
    w_dt = w_in[:, :, s0:s0 + wd_].astype(BF16)
    w_dt_pad = jnp.concatenate([w_dt, jnp.zeros((depth, d, LANES - n_heads), BF16)], axis=2)
    wb = {name: w.astype(BF16) for name, w in (
        ("br_att", w_br_att), ("br_ssm", w_br_ssm), ("mix_out", w_mix_out), ("xkv", w_xkv),
        ("xq", w_xq), ("xo", w_xo), ("gu", w_gu), ("down", w_down))}

    for l in range(depth):
        proj, dt, dtt = _inproj(x2, row(g_pre_mix[l]), w_perm, w_dt_pad, n_heads, l, tm_proj, tn)

        n_pairs = sbw // LANES
        gp = SB_PAIRS_PER_STEP if n_pairs % SB_PAIRS_PER_STEP == 0 else 1
        o_att = _sb_attention(proj, uu, bsz, seq, n_pairs, gp, col["q"] // (gp * LANES),
                              col["k"] // (gp * LANES), col["v"] // (gp * LANES))
        pad_row = lambda v: jnp.zeros((1, LANES), F32).at[0, :n_heads].set(v)
        prm = {
            "conv_w": conv_w[l], "conv_b": row(conv_b[l]),
            "dt_bias_row": pad_row(dt_bias[l]),
            "dt_bias_col": jnp.broadcast_to(dt_bias[l][:, None], (n_heads, SSM_CHUNK)),
            "a_log_row": pad_row(a_log[l]),
            "a_log_col": jnp.broadcast_to(a_log[l][:, None], (n_heads, SSM_CHUNK)),
            "d_skip": row(jnp.repeat(d_skip[l], SSM_HEAD_DIM)),
            "g_norm": row(g_ssm_norm[l]),
            "tri": tri, "trit": trit, "shift": shift,
        }
        o_ssm = _ssd(proj, dt, dtt, prm, bsz, seq, inner, n_heads, col["xbc"] // cw, col["z"] // inner)
        x2 = _merge(x2, o_att, o_ssm, proj, wb["br_att"], wb["br_ssm"], wb["mix_out"],
                    row(g_post_mix[l]), l, tm_row, col["ga"] // d, col["gs"] // d)

        kv = _norm_matmul(mem2, row(g_mem[l]), wb["xkv"], l, _pick_tile(bsz * mem_len, 512))
        x2 = _xattn(x2, kv, row(g_pre_xa[l]), wb["xq"], wb["xo"], row(g_post_xa[l]), l, tm_row,
                    seq, mem_len)

        x2 = _ffn(x2, row(g_pre_ffn[l]), wb["gu"], wb["down"], row(g_post_ffn[l]), l, tm_row)
    return x2.reshape(bsz, seq, d)
```

```python
import functools

import jax
import jax.numpy as jnp
from jax import lax
from jax.experimental import pallas as pl
from jax.experimental.pallas import tpu as pltpu

F32 = jnp.float32
BF16 = jnp.bfloat16

RMS_EPS = 1e-6
LOG2E = 1.4426950408889634
SB_HEAD_DIM = 64
SB_BLOCK = 128
SB_PAIRS_PER_STEP = 8
SB_DEAD_LOG = -float("inf")
SB_EARLY_ROWS = 64
SSM_HEAD_DIM = 64
SSM_GROUPS = 4
SSM_STATE = 128
SSM_CONV = 4
SSM_CHUNK = 128
XA_HEADS = 4
LANES = 128
VMEM_LIMIT_BYTES = 56 * 1024 * 1024


def _cparams(*sem):
    return pltpu.CompilerParams(dimension_semantics=sem, vmem_limit_bytes=VMEM_LIMIT_BYTES)


def _pick_tile(n, want):
    t = min(n, want)
    while n % t:
        t //= 2
    return t


def _resident(shape):
    return pl.BlockSpec(shape, lambda *_: (0,) * len(shape), pipeline_mode=pl.Buffered(1))


def _layer_resident(shape, layer):
    return pl.BlockSpec((None,) + tuple(shape), lambda *_: (layer,) + (0,) * len(shape),
                        pipeline_mode=pl.Buffered(1))


def _rms(x, g):
    ms = jnp.mean(x * x, axis=-1, keepdims=True)
    return x * lax.rsqrt(ms + RMS_EPS) * g


def _sigmoid(x):
    return 1.0 / (1.0 + jnp.exp2(x * (-LOG2E)))


def _split3(x):
    hi = x.astype(BF16)
    r = x - hi.astype(F32)
    mid = r.astype(BF16)
    lo = (r - mid.astype(F32)).astype(BF16)
    return hi, mid, lo


def _inproj_kernel(x_ref, g_ref, w_ref, wdt_ref, o_ref, dt_ref, dtt_ref, h_ref):
    @pl.when(pl.program_id(1) == 0)
    def _():
        h = _rms(x_ref[...], g_ref[...]).astype(BF16)
        h_ref[...] = h
        dt = jnp.dot(h, wdt_ref[...], preferred_element_type=F32)
        dt_ref[...] = dt
        dtt_ref[...] = dt.T[:dtt_ref.shape[0], :]

    o_ref[...] = jnp.dot(h_ref[...], w_ref[...], preferred_element_type=F32).astype(o_ref.dtype)


def _inproj(x2, g, w, wdt, nh, layer, tm, tn):
    t, d = x2.shape
    n = w.shape[2]
    return pl.pallas_call(
        _inproj_kernel,
        grid=(t // tm, n // tn),
        in_specs=[
            pl.BlockSpec((tm, d), lambda i, j: (i, 0)),
            pl.BlockSpec((1, d), lambda i, j: (0, 0)),
            pl.BlockSpec((None, d, tn), lambda i, j: (layer, 0, j)),
            pl.BlockSpec((None, d, LANES), lambda i, j: (layer, 0, 0)),
        ],
        out_specs=[
            pl.BlockSpec((tm, tn), lambda i, j: (i, j)),
            pl.BlockSpec((tm, LANES), lambda i, j: (i, 0)),
            pl.BlockSpec((nh, tm), lambda i, j: (0, i)),
        ],
        out_shape=[
            jax.ShapeDtypeStruct((t, n), BF16),
            jax.ShapeDtypeStruct((t, LANES), F32),
            jax.ShapeDtypeStruct((nh, t), F32),
        ],
        scratch_shapes=[pltpu.VMEM((tm, d), BF16)],
        compiler_params=_cparams("parallel", "arbitrary"),
        name="inproj",
    )(x2, g, w, wdt)


def _norm_matmul_kernel(x_ref, g_ref, w_ref, o_ref):
    h = _rms(x_ref[...], g_ref[...]).astype(BF16)
    o_ref[...] = jnp.dot(h, w_ref[...], preferred_element_type=F32).astype(o_ref.dtype)


def _norm_matmul(x2, g, w, layer, tm):
    t, d = x2.shape
    n = w.shape[2]
    return pl.pallas_call(
        _norm_matmul_kernel,
        grid=(t // tm,),
        in_specs=[
            pl.BlockSpec((tm, d), lambda i: (i, 0)),
            pl.BlockSpec((1, d), lambda i: (0, 0)),
            _layer_resident((d, n), layer),
        ],
        out_specs=pl.BlockSpec((tm, n), lambda i: (i, 0)),
        out_shape=jax.ShapeDtypeStruct((t, n), BF16),
        compiler_params=_cparams("parallel"),
        name="mem_kv",
    )(x2, g, w)


def _sb_kernel(q_ref, k_ref, v_ref, uu_ref, o_ref, qs_ref, acc_ref, run_ref, *, n_pairs):
    i = pl.program_id(2)
    blk = SB_BLOCK
    lane = lax.broadcasted_iota(jnp.int32, (blk, LANES), 1)
    first_head = lane < SB_HEAD_DIM
    row = lax.broadcasted_iota(jnp.int32, (2 * blk, blk), 0)
    col = lax.broadcasted_iota(jnp.int32, (2 * blk, blk), 1)
    causal = col < jnp.where(row >= blk, row - blk, row)

    for p in range(n_pairs):
        q = q_ref[:, p * LANES:(p + 1) * LANES] * (-(SB_HEAD_DIM ** -0.5))
        zero = jnp.zeros_like(q)
        qs_ref[p] = jnp.concatenate([jnp.where(first_head, q, zero), jnp.where(first_head, zero, q)],
                                    axis=0)

    hr = SB_EARLY_ROWS

    def take(ref, p, early_only):
        if not early_only:
            return ref[p]
        return jnp.concatenate([ref[p, 0:hr, :], ref[p, blk:blk + hr, :]], axis=0)

    def put(ref, p, val, early_only):
        if not early_only:
            ref[p] = val
        else:
            ref[p, 0:hr, :] = val[:hr]
            ref[p, blk:blk + hr, :] = val[hr:]

    def any_live(run_max):
        return (jnp.max(run_max) > SB_DEAD_LOG).astype(jnp.int32)

    def phase(j, diag, early_only):
        pairs = range(n_pairs)
        start = pl.multiple_of(j * blk, blk)
        zns = [lax.dot_general(take(qs_ref, p, early_only),
                               k_ref[pl.ds(start, blk), p * LANES:(p + 1) * LANES],
                               (((1,), (1,)), ((), ())), preferred_element_type=F32) for p in pairs]
        cs = []
        for p in pairs:
            zn = zns[p]
            log_1mb = jnp.minimum(zn, 0.0) - jnp.log(1.0 + jnp.exp2(jnp.abs(zn) * (-LOG2E)))
            if diag:
                log_1mb = jnp.where(causal, log_1mb, 0.0)
            hi = log_1mb.astype(BF16)
            lo = (log_1mb - hi.astype(F32)).astype(BF16)
            cs.append(jnp.dot(jnp.concatenate([hi, lo], axis=1), uu_ref[...],
                              preferred_element_type=F32))
        ws = []
        run_max = None
        for p in pairs:
            c = cs[p]
            if diag:
                tail = c[:, :blk]
                run = c[:, blk:]
            else:
                run = take(run_ref, p, early_only)
                tail = c[:, :blk] + run
                run = run + c[:, blk:]
            put(run_ref, p, run, early_only)
            run_max = run if run_max is None else jnp.maximum(run_max, run)
            w = jnp.exp2((tail - zns[p]) * LOG2E)
            if diag:
                w = jnp.where(causal, w, 0.0)
            ws.append(w.astype(BF16))
        for p in pairs:
            pv = jnp.dot(ws[p], v_ref[pl.ds(start, blk), p * LANES:(p + 1) * LANES],
                         preferred_element_type=F32)
            if diag:
                acc_ref[p] = pv
            else:
                put(acc_ref, p, take(acc_ref, p, early_only) + pv, early_only)
        if early_only:
            return any_live(run_max), jnp.int32(0)
        early = jnp.maximum(run_max[0:hr], run_max[blk:blk + hr])
        later = jnp.maximum(run_max[hr:blk], run_max[blk + hr:])
        return any_live(early), any_live(later)

    early_live, later_live = phase(i, True, False)

    def full_body(carry):
        jj, _, _ = carry
        return (jj + 1,) + phase(i - 1 - jj, False, False)

    jj, early_live, _ = lax.while_loop(lambda c: jnp.logical_and(c[0] < i, c[2] > 0), full_body,
                                       (jnp.int32(0), early_live, later_live))

    def early_body(carry):
        jj, _ = carry
        return jj + 1, phase(i - 1 - jj, False, True)[0]

    lax.while_loop(lambda c: jnp.logical_and(c[0] < i, c[1] > 0), early_body, (jj, early_live))
    for p in range(n_pairs):
        acc = acc_ref[p]
        o_ref[:, p * LANES:(p + 1) * LANES] = jnp.where(first_head, acc[:blk], acc[blk:]).astype(o_ref.dtype)


def _sb_attention(proj, uu, bsz, seq, n_pairs, pairs_per_step, q_col, k_col, v_col):
    nq = seq // SB_BLOCK
    gp = pairs_per_step
    width = gp * LANES
    return pl.pallas_call(
        functools.partial(_sb_kernel, n_pairs=gp),
        grid=(bsz, n_pairs // gp, nq),
        in_specs=[
            pl.BlockSpec((SB_BLOCK, width), lambda b, p, i: (b * nq + i, q_col + p)),
            pl.BlockSpec((seq, width), lambda b, p, i: (b, k_col + p)),
            pl.BlockSpec((seq, width), lambda b, p, i: (b, v_col + p)),
            _resident((2 * SB_BLOCK, 2 * SB_BLOCK)),
        ],
        out_specs=pl.BlockSpec((SB_BLOCK, width), lambda b, p, i: (b * nq + i, p)),
        out_shape=jax.ShapeDtypeStruct((bsz * seq, n_pairs * LANES), BF16),
        scratch_shapes=[pltpu.VMEM((gp, 2 * SB_BLOCK, LANES), BF16),
                        pltpu.VMEM((gp, 2 * SB_BLOCK, LANES), F32),
                        pltpu.VMEM((gp, 2 * SB_BLOCK, SB_BLOCK), F32)],
        compiler_params=_cparams("parallel", "parallel", "arbitrary"),
        name="sb_attention",
    )(proj, proj, proj, uu)


def _ssd_kernel(xbc_ref, z_ref, dt_ref, dtt_ref, convw_ref, convb_ref, dtb_ref, dtbt_ref,
                alog_ref, alogt_ref, dskip_ref, gnorm_ref, tri_ref, trit_ref, shift_ref,
                o_ref, state_ref, prev_ref, *, inner, n_heads):
    c = pl.program_id(1)
    cl = SSM_CHUNK
    gw = SSM_GROUPS * SSM_STATE
    hpg = n_heads // SSM_GROUPS

    @pl.when(c == 0)
    def _():
        state_ref[...] = jnp.zeros_like(state_ref)
        prev_ref[...] = jnp.zeros_like(prev_ref)

    cur = xbc_ref[...]
    halo = prev_ref[...]
    prev_ref[...] = cur[cl - 8:, :].astype(F32)

    def conv_silu(c0, width):
        cols = slice(c0, c0 + width)
        delayed = jnp.dot(shift_ref[...], cur[:, cols], preferred_element_type=F32)
        acc = cur[:, cols].astype(F32) * convw_ref[SSM_CONV - 1:SSM_CONV, cols] + convb_ref[:, cols]
        for k in range(1, SSM_CONV):
            d_k = delayed[(k - 1) * cl:k * cl]
            edge = lax.broadcasted_iota(jnp.int32, (8, width), 0) < k
            top = jnp.where(edge, pltpu.roll(halo[:, cols], k, 0), d_k[:8])
            d_k = jnp.concatenate([top, d_k[8:]], axis=0)
            acc = acc + d_k * convw_ref[SSM_CONV - 1 - k:SSM_CONV - k, cols]
        return acc * _sigmoid(acc)

    bm = conv_silu(inner, gw)
    cm = conv_silu(inner + gw, gw)

    def softplus(v):
        return jnp.maximum(v, 0.0) + jnp.log(1.0 + jnp.exp2(jnp.abs(v) * (-LOG2E)))

    dt_r = softplus(dt_ref[...] + dtb_ref[...])
    dt_t = softplus(dtt_ref[...] + dtbt_ref[...])
    adt_r = dt_r * (-jnp.exp(alog_ref[...]))
    adt_t = dt_t * (-jnp.exp(alogt_ref[...]))
    acs_r = sum(jnp.dot(tri_ref[...], p, preferred_element_type=F32) for p in _split3(adt_r)) * LOG2E
    acs_t = sum(jnp.dot(p, trit_ref[...], preferred_element_type=F32) for p in _split3(adt_t)) * LOG2E
    src_t = acs_t - jnp.log(dt_t) * LOG2E

    lane = lax.broadcasted_iota(jnp.int32, (cl, LANES), 1)
    first_head = lane < SSM_HEAD_DIM
    tl = lax.broadcasted_iota(jnp.int32, (cl, cl), 0)
    ts = lax.broadcasted_iota(jnp.int32, (cl, cl), 1)
    tril = ts <= tl

    gsz = inner // SSM_GROUPS
    for g in range(SSM_GROUPS):
        gcols = slice(g * gsz, (g + 1) * gsz)
        xs = conv_silu(g * gsz, gsz)
        y_pairs = []
        bg = bm[:, g * SSM_STATE:(g + 1) * SSM_STATE]
        cg = cm[:, g * SSM_STATE:(g + 1) * SSM_STATE]
        cb = lax.dot_general(cg.astype(BF16), bg.astype(BF16), (((1,), (1,)), ((), ())),
                             preferred_element_type=F32)
        bgt = bg.T
        for pi in range(hpg // 2):
            h0 = g * hpg + 2 * pi
            lhs_y, lhs_s, cdec = [], [], []
            for h in (h0, h0 + 1):
                a_col = acs_r[:, h:h + 1]
                s_row = src_t[h:h + 1, :]
                a_end = acs_t[h:h + 1, cl - 1:cl]
                m = cb * jnp.exp2(jnp.where(tril, a_col - s_row, -jnp.inf))
                c_scaled = cg * jnp.exp2(a_col)
                lhs_y.append(jnp.concatenate([m.astype(BF16), c_scaled.astype(BF16)], axis=1))
                w_row = jnp.exp2(a_end - s_row)
                lhs_s.append((bgt * w_row).astype(BF16))
                cdec.append(jnp.exp2(a_end))
            cols = slice(h0 * SSM_HEAD_DIM, (h0 + 2) * SSM_HEAD_DIM)
            x_pair = xs[:, 2 * pi * SSM_HEAD_DIM:(2 * pi + 2) * SSM_HEAD_DIM].astype(BF16)
            s_prev = state_ref[:, cols]
            rhs = jnp.concatenate([x_pair, s_prev.astype(BF16)], axis=0)
            out = jnp.dot(jnp.concatenate(lhs_y, axis=0), rhs, preferred_element_type=F32)
            y_pairs.append(jnp.where(first_head, out[:cl], out[cl:]))
            st = jnp.dot(jnp.concatenate(lhs_s, axis=0), x_pair, preferred_element_type=F32)
            st = jnp.where(first_head, st[:SSM_STATE], st[SSM_STATE:])
            state_ref[:, cols] = s_prev * jnp.where(first_head, cdec[0], cdec[1]) + st

        y = jnp.concatenate(y_pairs, axis=1) + dskip_ref[:, gcols] * xs
        zf = z_ref[:, gcols].astype(F32)
        y = y * (zf * _sigmoid(zf))
        y = y * lax.rsqrt(jnp.mean(y * y, axis=-1, keepdims=True) + RMS_EPS)
        o_ref[:, gcols] = (y * gnorm_ref[:, gcols]).astype(o_ref.dtype)


def _ssd(proj, dt, dtt, prm, bsz, seq, inner, n_heads, xbc_col, z_col):
    nc = seq // SSM_CHUNK
    cw = inner + 2 * SSM_GROUPS * SSM_STATE
    const = lambda shape: pl.BlockSpec(shape, lambda b, c: (0, 0))
    kern = functools.partial(_ssd_kernel, inner=inner, n_heads=n_heads)
    return pl.pallas_call(
        kern,
        grid=(bsz, nc),
        in_specs=[
            pl.BlockSpec((SSM_CHUNK, cw), lambda b, c: (b * nc + c, xbc_col)),
            pl.BlockSpec((SSM_CHUNK, inner), lambda b, c: (b * nc + c, z_col)),
            pl.BlockSpec((SSM_CHUNK, LANES), lambda b, c: (b * nc + c, 0)),
            pl.BlockSpec((n_heads, SSM_CHUNK), lambda b, c: (0, b * nc + c)),
            const((SSM_CONV, cw)),
            const((1, cw)),
            const((1, LANES)),
            const((n_heads, SSM_CHUNK)),
            const((1, LANES)),
            const((n_heads, SSM_CHUNK)),
            const((1, inner)),
            const((1, inner)),
            const((SSM_CHUNK, SSM_CHUNK)),
            const((SSM_CHUNK, SSM_CHUNK)),
            const(((SSM_CONV - 1) * SSM_CHUNK, SSM_CHUNK)),
        ],
        out_specs=pl.BlockSpec((SSM_CHUNK, inner), lambda b, c: (b * nc + c, 0)),
        out_shape=jax.ShapeDtypeStruct((bsz * seq, inner), BF16),
        scratch_shapes=[pltpu.VMEM((SSM_STATE, inner), F32), pltpu.VMEM((8, cw), F32)],
        compiler_params=_cparams("parallel", "arbitrary"),
        name="ssd",
    )(proj, proj, dt, dtt, prm["conv_w"], prm["conv_b"], prm["dt_bias_row"], prm["dt_bias_col"],
      prm["a_log_row"], prm["a_log_col"], prm["d_skip"], prm["g_norm"], prm["tri"], prm["trit"],
      prm["shift"])


def _merge_kernel(x_ref, oa_ref, os_ref, ga_ref, gs_ref, wa_ref, ws_ref, wm_ref, g_ref, o_ref):
    a = jnp.dot(oa_ref[...], wa_ref[...], preferred_element_type=F32)
    s = jnp.dot(os_ref[...], ws_ref[...], preferred_element_type=F32)
    merged = _sigmoid(ga_ref[...].astype(F32)) * a + _sigmoid(gs_ref[...].astype(F32)) * s
    y = jnp.dot(merged.astype(BF16), wm_ref[...], preferred_element_type=F32)
    o_ref[...] = x_ref[...] + _rms(y, g_ref[...])


def _merge(x2, o_att, o_ssm, proj, wa, ws, wm, g, layer, tm, ga_col, gs_col):
    t, d = x2.shape
    inner = o_ssm.shape[1]
    sbw = o_att.shape[1]
    return pl.pallas_call(
        _merge_kernel,
        grid=(t // tm,),
        in_specs=[
            pl.BlockSpec((tm, d), lambda i: (i, 0)),
            pl.BlockSpec((tm, sbw), lambda i: (i, 0)),
            pl.BlockSpec((tm, inner), lambda i: (i, 0)),
            pl.BlockSpec((tm, d), lambda i: (i, ga_col)),
            pl.BlockSpec((tm, d), lambda i: (i, gs_col)),
            _layer_resident((sbw, d), layer),
            _layer_resident((inner, d), layer),
            _layer_resident((d, d), layer),
            _resident((1, d)),
        ],
        out_specs=pl.BlockSpec((tm, d), lambda i: (i, 0)),
        out_shape=jax.ShapeDtypeStruct((t, d), F32),
        compiler_params=_cparams("parallel"),
        name="merge",
    )(x2, o_att, o_ssm, proj, proj, wa, ws, wm, g)


def _xattn_kernel(x_ref, kv_ref, gpre_ref, wq_ref, wo_ref, gpost_ref, o_ref, *, d):
    x = x_ref[...]
    h = _rms(x, gpre_ref[...]).astype(BF16)
    hd = d // XA_HEADS
    q = (jnp.dot(h, wq_ref[...], preferred_element_type=F32) * (hd ** -0.5)).astype(BF16)
    heads = range(XA_HEADS)
    scores = [lax.dot_general(q[:, n * hd:(n + 1) * hd], kv_ref[:, n * hd:(n + 1) * hd],
                              (((1,), (1,)), ((), ())), preferred_element_type=F32)
              for n in heads]
    probs, denoms = [], []
    for n in heads:
        p = jnp.exp(scores[n] - jnp.max(scores[n], axis=-1, keepdims=True))
        denoms.append(jnp.sum(p, axis=-1, keepdims=True))
        probs.append(p.astype(BF16))
    outs = [(jnp.dot(probs[n], kv_ref[:, d + n * hd:d + (n + 1) * hd], preferred_element_type=F32)
             / denoms[n]).astype(BF16) for n in heads]
    o = jnp.concatenate(outs, axis=1)
    y = jnp.dot(o, wo_ref[...], preferred_element_type=F32)
    o_ref[...] = x + _rms(y, gpost_ref[...])


def _xattn(x2, kv, gpre, wq, wo, gpost, layer, tm, seq, mem_len):
    t, d = x2.shape
    per_b = seq // tm
    return pl.pallas_call(
        functools.partial(_xattn_kernel, d=d),
        grid=(t // tm,),
        in_specs=[
            pl.BlockSpec((tm, d), lambda i: (i, 0)),
            pl.BlockSpec((mem_len, 2 * d), lambda i: (i // per_b, 0)),
            _resident((1, d)),
            _layer_resident((d, d), layer),
            _layer_resident((d, d), layer),
            _resident((1, d)),
        ],
        out_specs=pl.BlockSpec((tm, d), lambda i: (i, 0)),
        out_shape=jax.ShapeDtypeStruct((t, d), F32),
        compiler_params=_cparams("parallel"),
        name="xattn",
    )(x2, kv, gpre, wq, wo, gpost)


def _ffn_kernel(x_ref, gpre_ref, wgu_ref, wd_ref, gpost_ref, o_ref, *, hidden, chunk):
    x = x_ref[...]
    h = _rms(x, gpre_ref[...]).astype(BF16)
    y = jnp.zeros(x.shape, F32)
    for c0 in range(0, hidden, chunk):
        gate = jnp.dot(h, wgu_ref[:, c0:c0 + chunk], preferred_element_type=F32)
        up = jnp.dot(h, wgu_ref[:, hidden + c0:hidden + c0 + chunk], preferred_element_type=F32)
        act = (gate * _sigmoid(gate) * up).astype(BF16)
        y = y + jnp.dot(act, wd_ref[c0:c0 + chunk, :], preferred_element_type=F32)
    o_ref[...] = x + _rms(y, gpost_ref[...])


def _ffn(x2, gpre, wgu, wd, gpost, layer, tm):
    t, d = x2.shape
    hidden = wd.shape[1]
    chunk = hidden
    for cand in (512, 384, 256, 128):
        if hidden % cand == 0:
            chunk = cand
            break
    return pl.pallas_call(
        functools.partial(_ffn_kernel, hidden=hidden, chunk=chunk),
        grid=(t // tm,),
        in_specs=[
            pl.BlockSpec((tm, d), lambda i: (i, 0)),
            _resident((1, d)),
            _layer_resident((d, 2 * hidden), layer),
            _layer_resident((hidden, d), layer),
            _resident((1, d)),
        ],
        out_specs=pl.BlockSpec((tm, d), lambda i: (i, 0)),
        out_shape=jax.ShapeDtypeStruct((t, d), F32),
        compiler_params=_cparams("parallel"),
        name="ffn",
    )(x2, gpre, wgu, wd, gpost)


def _segment_layout(widths, tile):
    offs, cur = [], 0
    for w in widths:
        cur = -(-cur // w) * w
        offs.append(cur)
        cur += w
    return offs, -(-cur // tile) * tile


def kernel(x, mem, g_pre_mix, w_in, conv_w, conv_b, dt_bias, a_log, d_skip, g_ssm_norm, w_br_att,
           w_br_ssm, w_mix_out, g_post_mix, g_pre_xa, g_mem, w_xq, w_xkv, w_xo, g_post_xa,
           g_pre_ffn, w_gu, w_down, g_post_ffn):
    bsz, seq, d = x.shape
    depth = w_in.shape[0]
    mem_len = mem.shape[1]
    sbw = w_br_att.shape[1]
    inner = w_br_ssm.shape[1]
    n_heads = inner // SSM_HEAD_DIM
    gw = SSM_GROUPS * SSM_STATE
    cw = inner + 2 * gw
    t = bsz * seq
    assert seq % SB_BLOCK == 0 and seq % SSM_CHUNK == 0 and sbw % LANES == 0
    assert n_heads % (2 * SSM_GROUPS) == 0 and n_heads <= LANES and (d // XA_HEADS) % LANES == 0

    src = {}
    cur = 0
    for name, wd_ in (("q", sbw), ("k", sbw), ("v", sbw), ("z", inner), ("xbc", cw),
                      ("dt", n_heads), ("ga", d), ("gs", d)):
        src[name] = (cur, wd_)
        cur += wd_
    order = ("q", "k", "v", "xbc", "z", "ga", "gs")
    tn = 1024
    offs, n_total = _segment_layout([src[k][1] for k in order], tn)
    col = dict(zip(order, offs))

    tm_proj = _pick_tile(t, 2048)
    tm_row = _pick_tile(seq, 512)

    ti = lax.broadcasted_iota(jnp.int32, (SSM_CHUNK, SSM_CHUNK), 0)
    tj = lax.broadcasted_iota(jnp.int32, (SSM_CHUNK, SSM_CHUNK), 1)
    tri = (tj <= ti).astype(BF16)
    trit = (ti <= tj).astype(BF16)
    suffix = (ti >= tj).astype(BF16)
    half = jnp.concatenate([suffix, jnp.ones((SB_BLOCK, SB_BLOCK), BF16)], axis=1)
    uu = jnp.concatenate([half, half], axis=0)
    shift = jnp.concatenate([(tj == ti - k).astype(BF16) for k in range(1, SSM_CONV)], axis=0)

    x2 = x.reshape(t, d)
    mem2 = mem.reshape(bsz * mem_len, d)
    row = lambda v: v.reshape(1, -1)

    pieces, cur = [], 0
    for k in order:
        s0, wd_ = src[k]
        if col[k] > cur:
            pieces.append(jnp.zeros((depth, d, col[k] - cur), BF16))
        pieces.append(w_in[:, :, s0:s0 + wd_].astype(BF16))
        cur = col[k] + wd_
    if n_total > cur:
        pieces.append(jnp.zeros((depth, d, n_total - cur), BF16))
    w_perm = jnp.concatenate(pieces, axis=2)
    s0, wd_ = src["dt"]
    w_dt = w_in[:, :, s0:s0 + wd_].astype(BF16)
    w_dt_pad = jnp.concatenate([w_dt, jnp.zeros((depth, d, LANES - n_heads), BF16)], axis=2)
    wb = {name: w.astype(BF16) for name, w in (
        ("br_att", w_br_att), ("br_ssm", w_br_ssm), ("mix_out", w_mix_out), ("xkv", w_xkv),
        ("xq", w_xq), ("xo", w_xo), ("gu", w_gu), ("down", w_down))}

    for l in range(depth):
        proj, dt, dtt = _inproj(x2, row(g_pre_mix[l]), w_perm, w_dt_pad, n_heads, l, tm_proj, tn)

        n_pairs = sbw // LANES
        gp = SB_PAIRS_PER_STEP if n_pairs % SB_PAIRS_PER_STEP == 0 else 1
        o_att = _sb_attention(proj, uu, bsz, seq, n_pairs, gp, col["q"] // (gp * LANES),
                              col["k"] // (gp * LANES), col["v"] // (gp * LANES))
        pad_row = lambda v: jnp.zeros((1, LANES), F32).at[0, :n_heads].set(v)
        prm = {
            "conv_w": conv_w[l], "conv_b": row(conv_b[l]),
            "dt_bias_row": pad_row(dt_bias[l]),
            "dt_bias_col": jnp.broadcast_to(dt_bias[l][:, None], (n_heads, SSM_CHUNK)),
            "a_log_row": pad_row(a_log[l]),
            "a_log_col": jnp.broadcast_to(a_log[l][:, None], (n_heads, SSM_CHUNK)),
            "d_skip": row(jnp.repeat(d_skip[l], SSM_HEAD_DIM)),
            "g_norm": row(g_ssm_norm[l]),
            "tri": tri, "trit": trit, "shift": shift,
        }
        o_ssm = _ssd(proj, dt, dtt, prm, bsz, seq, inner, n_heads, col["xbc"] // cw, col["z"] // inner)
        x2 = _merge(x2, o_att, o_ssm, proj, wb["br_att"], wb["br_ssm"], wb["mix_out"],
                    row(g_post_mix[l]), l, tm_row, col["ga"] // d, col["gs"] // d)

        kv = _norm_matmul(mem2, row(g_mem[l]), wb["xkv"], l, _pick_tile(bsz * mem_len, 512))
        x2 = _xattn(x2, kv, row(g_pre_xa[l]), wb["xq"], wb["xo"], row(g_post_xa[l]), l, tm_row,
                    seq, mem_len)

        x2 = _ffn(x2, row(g_pre_ffn[l]), wb["gu"], wb["down"], row(g_post_ffn[l]), l, tm_row)
    return x2.reshape(bsz, seq, d)
```
